```python
import jax, jax.numpy as jnp
from jax import lax
import numpy as np

D_MODEL = 1024
BATCH = 16
SEQ = 256
DEPTH = 4
DEC_BATCH = 4
DEC_SEQ = 4096
PAST_LEN = 512

GRID_W = 64
N_MIXERS = 2
N_MLA_LAYERS = (DEPTH + 1) // 2
N_NA_LAYERS = DEPTH // 2

MLA_HEADS = 16
MLA_NOPE_DIM = 64
MLA_ROPE_DIM = 32
MLA_V_DIM = 64
MLA_QK_DIM = MLA_NOPE_DIM + MLA_ROPE_DIM
MLA_Q_RANK = 256
MLA_KV_RANK = 128

NA_HEADS = 16
NA_HEAD_DIM = D_MODEL // NA_HEADS
NA_WIN_R = 8
NA_WIN_C = 16

PEER_HEADS = 8
PEER_KEYS = 128
PEER_EXPERTS = PEER_KEYS * PEER_KEYS
PEER_KEY_DIM = 128
PEER_TOPK = 16
PEER_CHUNK = 128

Q_BLOCK = 128
ROPE_BASE = 10000.0
EPS = 1e-6

kernel_name = "hybrid_mla_natten_peer_diffusion_step"


def rms_norm(x, g):
    x32 = x.astype(jnp.float32)
    y = x32 * lax.rsqrt(jnp.mean(x32 * x32, axis=-1, keepdims=True) + EPS)
    return (y * g.astype(jnp.float32)).astype(x.dtype)


def ada_modulation(cond, w_mod, b_mod):
    m = jax.nn.silu(cond) @ w_mod + b_mod
    return jnp.split(m[:, None, :], 6, axis=-1)


def modulate(x, g, shift, scale):
    return rms_norm(x, g) * (1 + scale) + shift


def axial_rope_tables(n_tok, dim, dtype):
    t = jnp.arange(n_tok)
    row = (t // GRID_W).astype(jnp.float32)
    col = (t % GRID_W).astype(jnp.float32)
    half = dim // 2
    inv = 1.0 / (ROPE_BASE ** (jnp.arange(0, half, 2, dtype=jnp.float32) / half))
    ar = row[:, None] * inv[None, :]
    ac = col[:, None] * inv[None, :]
    ang = jnp.concatenate([ar, ar, ac, ac], axis=-1)
    return jnp.cos(ang).astype(dtype), jnp.sin(ang).astype(dtype)


def rotate_half_axial(x):
    a, b, c, d = jnp.split(x, 4, axis=-1)
    return jnp.concatenate([-b, a, -d, c], axis=-1)


def rope_tail(x, cos, sin):
    xn, xr = x[..., :MLA_NOPE_DIM], x[..., MLA_NOPE_DIM:]
    cs, sn = cos[None, :, None, :], sin[None, :, None, :]
    return jnp.concatenate([xn, xr * cs + rotate_half_axial(xr) * sn], axis=-1)


def blocked_attention(q, k, v, scale):
    B, H, Tq, dk = q.shape
    nb = Tq // Q_BLOCK
    qb = q.reshape(B, H, nb, Q_BLOCK, dk).transpose(2, 0, 1, 3, 4)

    def attend(q_blk):
        s = jnp.einsum("bhqd,bhkd->bhqk", q_blk, k).astype(jnp.float32) * scale
        p = jax.nn.softmax(s, axis=-1).astype(v.dtype)
        return jnp.einsum("bhqk,bhkd->bhqd", p, v)

    o = lax.map(attend, qb)
    return o.transpose(1, 2, 0, 3, 4).reshape(B, H, Tq, v.shape[-1])


def merge_heads(o, w_o):
    B, H, T, dv = o.shape
    return o.transpose(0, 2, 1, 3).reshape(B, T, H * dv) @ w_o


def mla_project(h, w_in, g_q_a, g_kv_a, w_q_b, g_q):
    B, T, _ = h.shape
    a = h @ w_in
    q_a, ckv, k_rope = jnp.split(a, [MLA_Q_RANK, MLA_Q_RANK + MLA_KV_RANK], axis=-1)
    q = (rms_norm(q_a, g_q_a) @ w_q_b).reshape(B, T, MLA_HEADS, MLA_QK_DIM)
    return rms_norm(q, g_q), rms_norm(ckv, g_kv_a), k_rope


def mla_keys_values(ckv_n, k_rope, w_kv_b, g_k):
    B, T, _ = ckv_n.shape
    kv = (ckv_n @ w_kv_b).reshape(B, T, MLA_HEADS, MLA_NOPE_DIM + MLA_V_DIM)
    k_nope, v = kv[..., :MLA_NOPE_DIM], kv[..., MLA_NOPE_DIM:]
    k_r = jnp.broadcast_to(k_rope[:, :, None, :], (B, T, MLA_HEADS, MLA_ROPE_DIM))
    k = rms_norm(jnp.concatenate([k_nope, k_r], axis=-1), g_k)
    return k, v


def mla_context(h, w_in, g_q_a, g_kv_a, w_q_b, w_kv_b, g_q, g_k, w_o):
    q, ckv_n, k_rope = mla_project(h, w_in, g_q_a, g_kv_a, w_q_b, g_q)
    k, v = mla_keys_values(ckv_n, k_rope, w_kv_b, g_k)
    o = blocked_attention(q.transpose(0, 2, 1, 3), k.transpose(0, 2, 1, 3),
                          v.transpose(0, 2, 1, 3), MLA_QK_DIM ** -0.5)
    return merge_heads(o, w_o), ckv_n, k_rope


def mla_latent(h, ctx_ckv, ctx_krope, w_in, g_q_a, g_kv_a, w_q_b, w_kv_b, g_q, g_k, w_o):
    B, T, _ = h.shape
    q, ckv_n, k_rope = mla_project(h, w_in, g_q_a, g_kv_a, w_q_b, g_q)
    cos, sin = axial_rope_tables(T, MLA_ROPE_DIM, h.dtype)
    q = rope_tail(q, cos, sin)
    k_lat, v_lat = mla_keys_values(ckv_n, k_rope, w_kv_b, g_k)
    k_lat = rope_tail(k_lat, cos, sin)
    k_ctx, v_ctx = mla_keys_values(ctx_ckv, ctx_krope, w_kv_b, g_k)
    k = jnp.concatenate([k_ctx, k_lat], axis=1).transpose(0, 2, 1, 3)
    v = jnp.concatenate([v_ctx, v_lat], axis=1).transpose(0, 2, 1, 3)
    o = blocked_attention(q.transpose(0, 2, 1, 3), k, v, MLA_QK_DIM ** -0.5)
    return merge_heads(o, w_o)


def na_project(h, w_in, g_q, g_k):
    B, T, _ = h.shape
    qkv = (h @ w_in).reshape(B, T, 3, NA_HEADS, NA_HEAD_DIM)
    q = rms_norm(qkv[:, :, 0], g_q).transpose(0, 2, 1, 3)
    k = rms_norm(qkv[:, :, 1], g_k).transpose(0, 2, 1, 3)
    v = qkv[:, :, 2].transpose(0, 2, 1, 3)
    return q, k, v


def na_context(h, w_in, g_q, g_k, w_o):
    q, k, v = na_project(h, w_in, g_q, g_k)
    o = blocked_attention(q, k, v, NA_HEAD_DIM ** -0.5)
    return merge_heads(o, w_o), k, v


def na_latent(h, k_ctx, v_ctx, w_in, g_q, g_k, rel_bias, w_o):
    B, T, _ = h.shape
    rows = T // GRID_W
    wr = min(NA_WIN_R, rows)
    wc = NA_WIN_C
    n_ctx = k_ctx.shape[2]
    scale = NA_HEAD_DIM ** -0.5
    q, k, v = na_project(h, w_in, g_q, g_k)
    q_rows = q.reshape(B, NA_HEADS, rows, GRID_W, NA_HEAD_DIM).transpose(2, 0, 1, 3, 4)
    k_grid = k.reshape(B, NA_HEADS, rows, GRID_W, NA_HEAD_DIM)
    v_grid = v.reshape(B, NA_HEADS, rows, GRID_W, NA_HEAD_DIM)
    cols = jnp.arange(GRID_W)
    col_start = jnp.clip(cols - wc // 2, 0, GRID_W - wc)
    col_idx = col_start[:, None] + jnp.arange(wc)[None, :]
    dc = col_idx - cols[:, None] + (NA_WIN_C - 1)
    bias_c = rel_bias[:, :, dc].transpose(0, 2, 1, 3)

    def row_attend(args):
        q_row, r = args
        rs = jnp.clip(r - wr // 2, 0, rows - wr)
        k_win = lax.dynamic_slice_in_dim(k_grid, rs, wr, axis=2)[:, :, :, col_idx]
        v_win = lax.dynamic_slice_in_dim(v_grid, rs, wr, axis=2)[:, :, :, col_idx]
        dr = rs + jnp.arange(wr) - r + (NA_WIN_R - 1)
        bias = bias_c[:, :, dr].astype(jnp.float32)
        s_win = jnp.einsum("bhcd,bhicjd->bhcij", q_row, k_win).astype(jnp.float32) * scale + bias[None]
        s_ctx = jnp.einsum("bhcd,bhld->bhcl", q_row, k_ctx).astype(jnp.float32) * scale
        s = jnp.concatenate([s_ctx, s_win.reshape(B, NA_HEADS, GRID_W, wr * wc)], axis=-1)
        p = jax.nn.softmax(s, axis=-1).astype(v_win.dtype)
        p_ctx = p[..., :n_ctx]
        p_win = p[..., n_ctx:].reshape(B, NA_HEADS, GRID_W, wr, wc)
        return (jnp.einsum("bhcl,bhld->bhcd", p_ctx, v_ctx)
                + jnp.einsum("bhcij,bhicjd->bhcd", p_win, v_win))

    o = lax.map(row_attend, (q_rows, jnp.arange(rows)))
    o = o.transpose(1, 2, 0, 3, 4).reshape(B, NA_HEADS, T, NA_HEAD_DIM)
    return merge_heads(o, w_o)


def peer(h, w_q, sub_keys, u_tab, v_tab):
    B, T, D = h.shape
    hf = h.reshape(-1, PEER_CHUNK, D)

    def chunk(hc):
        q = (hc @ w_q).reshape(PEER_CHUNK, PEER_HEADS, 2, PEER_KEY_DIM)
        s = jnp.einsum("chpd,hpnd->chpn", q, sub_keys).astype(jnp.float32)
        sv, si = lax.top_k(s, PEER_TOPK)
        comb = (sv[:, :, 0, :, None] + sv[:, :, 1, None, :]).reshape(PEER_CHUNK, PEER_HEADS, PEER_TOPK * PEER_TOPK)
        cidx = (si[:, :, 0, :, None] * PEER_KEYS + si[:, :, 1, None, :]).reshape(PEER_CHUNK, PEER_HEADS, PEER_TOPK * PEER_TOPK)
        top, pos = lax.top_k(comb, PEER_TOPK)
        expert = jnp.take_along_axis(cidx, pos, axis=-1)
        g = jax.nn.softmax(top, axis=-1).astype(hc.dtype)
        act = jax.nn.gelu(jnp.einsum("cd,chkd->chk", hc, u_tab[expert]), approximate=False)
        return jnp.einsum("chk,chkd->cd", g * act, v_tab[expert])

    return lax.map(chunk, hf).reshape(B, T, D)


def setup_inputs(seed: int = 0) -> dict:
    key = jax.random.key(seed)
    ks = iter(jax.random.split(key, 32))
    D = D_MODEL

    def nrm(shape, scale=1.0):
        return jax.random.normal(next(ks), shape, jnp.float32) * scale

    def gain(shape):
        return 1.0 + 0.05 * jax.random.normal(next(ks), shape, jnp.float32)

    return {
        "x_prompt": nrm((BATCH, SEQ, D)),
        "x_sample": nrm((DEC_BATCH, DEC_SEQ, D)),
        "cache_mla_ckv": nrm((DEC_BATCH, N_MLA_LAYERS, PAST_LEN, MLA_KV_RANK)),
        "cache_mla_krope": nrm((DEC_BATCH, N_MLA_LAYERS, PAST_LEN, MLA_ROPE_DIM)),
        "cache_na_k": nrm((DEC_BATCH, N_NA_LAYERS, NA_HEADS, PAST_LEN, NA_HEAD_DIM)),
        "cache_na_v": nrm((DEC_BATCH, N_NA_LAYERS, NA_HEADS, PAST_LEN, NA_HEAD_DIM)),
        "c": nrm((DEC_BATCH, D)),
        "c_ctx": nrm((D,)),
        "w_mod": nrm((DEPTH, D, 6 * D), 0.5 * D ** -0.5),
        "b_mod": nrm((DEPTH, 6 * D), 0.02),
        "g_norm_mix": gain((DEPTH, D)),
        "g_norm_ffn": gain((DEPTH, D)),
        "mla_w_in": nrm((N_MLA_LAYERS, D, MLA_Q_RANK + MLA_KV_RANK + MLA_ROPE_DIM), D ** -0.5),
        "mla_g_q_a": gain((N_MLA_LAYERS, MLA_Q_RANK)),
        "mla_g_kv_a": gain((N_MLA_LAYERS, MLA_KV_RANK)),
        "mla_w_q_b": nrm((N_MLA_LAYERS, MLA_Q_RANK, MLA_HEADS * MLA_QK_DIM), MLA_Q_RANK ** -0.5),
        "mla_w_kv_b": nrm((N_MLA_LAYERS, MLA_KV_RANK, MLA_HEADS * (MLA_NOPE_DIM + MLA_V_DIM)), MLA_KV_RANK ** -0.5),
        "mla_g_q": gain((N_MLA_LAYERS, MLA_QK_DIM)),
        "mla_g_k": gain((N_MLA_LAYERS, MLA_QK_DIM)),
        "mla_w_o": nrm((N_MLA_LAYERS, MLA_HEADS * MLA_V_DIM, D), (MLA_HEADS * MLA_V_DIM) ** -0.5),
        "na_w_in": nrm((N_NA_LAYERS, D, 3 * D), D ** -0.5),
        "na_g_q": gain((N_NA_LAYERS, NA_HEAD_DIM)),
        "na_g_k": gain((N_NA_LAYERS, NA_HEAD_DIM)),
        "na_rel_bias": nrm((N_NA_LAYERS, NA_HEADS, 2 * NA_WIN_R - 1, 2 * NA_WIN_C - 1), 0.2),
        "na_w_o": nrm((N_NA_LAYERS, D, D), D ** -0.5),
        "peer_w_q": nrm((DEPTH, D, PEER_HEADS * 2 * PEER_KEY_DIM), D ** -0.5),
        "peer_sub_keys": nrm((DEPTH, PEER_HEADS, 2, PEER_KEYS, PEER_KEY_DIM), PEER_KEY_DIM ** -0.5),
        "peer_u": nrm((DEPTH, PEER_EXPERTS, D), D ** -0.5),
        "peer_v": nrm((DEPTH, PEER_EXPERTS, D), PEER_HEADS ** -0.5),
    }


def reference(x_prompt, x_sample, cache_mla_ckv, cache_mla_krope, cache_na_k, cache_na_v, c, c_ctx,
              w_mod, b_mod, g_norm_mix, g_norm_ffn,
              mla_w_in, mla_g_q_a, mla_g_kv_a, mla_w_q_b, mla_w_kv_b, mla_g_q, mla_g_k, mla_w_o,
              na_w_in, na_g_q, na_g_k, na_rel_bias, na_w_o,
              peer_w_q, peer_sub_keys, peer_u, peer_v):
    y_ctx, y_lat = x_prompt, x_sample
    new_ckv, new_krope, new_k, new_v = [], [], [], []
    for i in range(DEPTH):
        j = i // N_MIXERS
        sh_a_c, sc_a_c, ga_a_c, sh_f_c, sc_f_c, ga_f_c = ada_modulation(c_ctx[None, :], w_mod[i], b_mod[i])
        sh_a_l, sc_a_l, ga_a_l, sh_f_l, sc_f_l, ga_f_l = ada_modulation(c, w_mod[i], b_mod[i])
        h_ctx = modulate(y_ctx, g_norm_mix[i], sh_a_c, sc_a_c)
        h_lat = modulate(y_lat, g_norm_mix[i], sh_a_l, sc_a_l)
        if i % N_MIXERS == 0:
            o_ctx, ckv_n, k_rope = mla_context(h_ctx, mla_w_in[j], mla_g_q_a[j], mla_g_kv_a[j], mla_w_q_b[j],
                                               mla_w_kv_b[j], mla_g_q[j], mla_g_k[j], mla_w_o[j])
            o_lat = mla_latent(h_lat, cache_mla_ckv[:, j], cache_mla_krope[:, j], mla_w_in[j], mla_g_q_a[j],
                               mla_g_kv_a[j], mla_w_q_b[j], mla_w_kv_b[j], mla_g_q[j], mla_g_k[j], mla_w_o[j])
            new_ckv.append(ckv_n)
            new_krope.append(k_rope)
        else:
            o_ctx, k_c, v_c = na_context(h_ctx, na_w_in[j], na_g_q[j], na_g_k[j], na_w_o[j])
            o_lat = na_latent(h_lat, cache_na_k[:, j], cache_na_v[:, j], na_w_in[j], na_g_q[j], na_g_k[j],
                              na_rel_bias[j], na_w_o[j])
            new_k.append(k_c)
            new_v.append(v_c)
        y_ctx = y_ctx + ga_a_c * o_ctx
        y_lat = y_lat + ga_a_l * o_lat
        y_ctx = y_ctx + ga_f_c * peer(modulate(y_ctx, g_norm_ffn[i], sh_f_c, sc_f_c),
                                      peer_w_q[i], peer_sub_keys[i], peer_u[i], peer_v[i])
        y_lat = y_lat + ga_f_l * peer(modulate(y_lat, g_norm_ffn[i], sh_f_l, sc_f_l),
                                      peer_w_q[i], peer_sub_keys[i], peer_u[i], peer_v[i])
    new_mla_ckv = jnp.stack(new_ckv, axis=1)
    new_mla_krope = jnp.stack(new_krope, axis=1)
    new_na_k = jnp.stack(new_k, axis=1)
    new_na_v = jnp.stack(new_v, axis=1)
    return (y_ctx, y_lat, new_mla_ckv, new_mla_krope, new_na_k, new_na_v)
```

```python
import functools

import numpy as np
import jax
import jax.numpy as jnp
from jax import lax
from jax.experimental import pallas as pl
from jax.experimental.pallas import tpu as pltpu

F32 = jnp.float32
BF16 = jnp.bfloat16

D_MODEL = 1024
GRID_W = 64
MLA_HEADS = 16
MLA_NOPE = 64
MLA_ROPE = 32
MLA_QK = MLA_NOPE + MLA_ROPE
MLA_Q_RANK = 256
MLA_KV_RANK = 128
NA_HEADS = 16
NA_HD = 64
NA_WIN_R = 8
NA_WIN_C = 16
PEER_HEADS = 8
PEER_KEYS = 128
PEER_TOPK = 16
ROPE_BASE = 10000.0
EPS = 1e-6

LANES = 128
TOKEN_TILE = 256
NA_GROUP_ROWS = 4
NA_SLAB_ROWS = NA_WIN_R + NA_GROUP_ROWS
PEER_EXPERT_BLOCK = 2048
PEER_NTOP = PEER_TOPK + 1
NEG_BIG = -1e30
VMEM_LIMIT = 56 * 1024 * 1024

_NT = (((1,), (1,)), ((), ()))


def _cparams(sem):
    return pltpu.CompilerParams(dimension_semantics=sem, vmem_limit_bytes=VMEM_LIMIT)


def _modulate(y, g, shift, scale):
    ms = jnp.mean(y * y, axis=-1, keepdims=True)
    return (y * lax.rsqrt(ms + EPS)) * g * (1.0 + scale) + shift


def _rms(x, n):
    return x * lax.rsqrt(jnp.sum(x * x, axis=-1, keepdims=True) * (1.0 / n) + EPS)


def _split_bf16(x):
    hi = x.astype(BF16)
    lo = (x - hi.astype(F32)).astype(BF16)
    return hi, lo


def _adaln_kernel(c_ref, w_ref, b_ref, o_ref):
    c = c_ref[...]
    s = c / (1.0 + jnp.exp(-c))
    o_ref[...] = jnp.dot(s, w_ref[...], preferred_element_type=F32,
                         precision=lax.Precision.HIGHEST) + b_ref[...]


def _adaln(cond8, w_mod, b_mod):
    depth, d, d6 = w_mod.shape
    nj = d6 // d
    return pl.pallas_call(
        _adaln_kernel,
        out_shape=jax.ShapeDtypeStruct((depth, 8, d6), F32),
        grid=(depth, nj),
        in_specs=[pl.BlockSpec((8, d), lambda l, j: (0, 0)),
                  pl.BlockSpec((None, d, d), lambda l, j: (l, 0, j)),
                  pl.BlockSpec((None, 1, d), lambda l, j: (l, 0, j))],
        out_specs=pl.BlockSpec((None, 8, d), lambda l, j: (l, 0, j)),
        compiler_params=_cparams(("parallel", "parallel")),
        name="adaln",
    )(cond8, w_mod, b_mod.reshape(depth, 1, d6))


class _Tokens:
    def __init__(self, batch, seq, dec_batch, dec_seq):
        self.batch, self.seq, self.dec_batch, self.dec_seq = batch, seq, dec_batch, dec_seq
        self.nc = batch * seq
        self.nl = dec_batch * dec_seq
        self.n = self.nc + self.nl
        tm = TOKEN_TILE
        assert seq % tm == 0 and dec_seq % tm == 0
        self.tm = tm
        self.nct = self.nc // tm
        self.tpb = dec_seq // tm
        self.nt = self.n // tm
        self.ctx_row = dec_batch

    def cond_row(self, t):
        lat = lax.div(jnp.maximum(t - self.nct, 0), self.tpb)
        return jnp.where(t < self.nct, self.ctx_row, lat)

    def rope_blk(self, t):
        lat = lax.rem(jnp.maximum(t - self.nct, 0), self.tpb)
        return jnp.where(t < self.nct, self.tpb, lat)


def _tok_spec(tk, width):
    return pl.BlockSpec((tk.tm, width), lambda t: (t, 0))


def _mod_spec(tk):
    return pl.BlockSpec((None, 6, D_MODEL), lambda t: (tk.cond_row(t), 0, 0))


def _const_spec(shape):
    nd = len(shape)
    return pl.BlockSpec(shape, lambda t: (0,) * nd)


def _rope(x, cos, sina, sinb):
    return x * cos + pltpu.roll(x, LANES - 8, 1) * sina + pltpu.roll(x, 8, 1) * sinb


def _mla_kv_tail(ckv_n, kr, wkn_ref, wv_ref, gk_ref, rope, k_ref, v_ref):
    cb = ckv_n.astype(BF16)
    kn = jnp.dot(cb, wkn_ref[...], preferred_element_type=F32)
    v_ref[...] = jnp.dot(cb, wv_ref[...], preferred_element_type=F32).astype(BF16)
    gk = gk_ref[...]
    for hd in range(MLA_HEADS):
        sl = slice(hd * LANES, (hd + 1) * LANES)
        kh = _rms(kn[:, sl] + kr, MLA_QK) * gk
        if rope is not None:
            kh = _rope(kh, *rope)
        k_ref[:, sl] = kh.astype(BF16)


def _mla_proj_kernel(y_ref, mod_ref, gmix_ref, win_ref, gqa_ref, gkva_ref, wqb_ref, gq_ref,
                     wkn_ref, wv_ref, gk_ref, cos_ref, sina_ref, sinb_ref,
                     q_ref, k_ref, v_ref, ckv_ref, kr_ref):
    h = _modulate(y_ref[...], gmix_ref[...], mod_ref[0:1, :], mod_ref[1:2, :])
    a = jnp.dot(h.astype(BF16), win_ref[...], preferred_element_type=F32)
    qa = a[:, :MLA_Q_RANK]
    ckv = a[:, MLA_Q_RANK:MLA_Q_RANK + MLA_KV_RANK]
    kr = a[:, MLA_Q_RANK + MLA_KV_RANK:]
    qn = _rms(qa, MLA_Q_RANK) * gqa_ref[...]
    ckv_n = _rms(ckv, MLA_KV_RANK) * gkva_ref[...]
    ckv_ref[...] = ckv_n
    kr_ref[...] = kr
    rope = (cos_ref[...], sina_ref[...], sinb_ref[...])
    q = jnp.dot(qn.astype(BF16), wqb_ref[...], preferred_element_type=F32)
    gq = gq_ref[...]
    for hd in range(MLA_HEADS):
        sl = slice(hd * LANES, (hd + 1) * LANES)
        qh = _rope(_rms(q[:, sl], MLA_QK) * gq, *rope)
        q_ref[:, sl] = qh.astype(BF16)
    _mla_kv_tail(ckv_n, kr, wkn_ref, wv_ref, gk_ref, rope, k_ref, v_ref)


def _mla_cache_kernel(ckv_ref, kr_ref, wkn_ref, wv_ref, gk_ref, k_ref, v_ref):
    _mla_kv_tail(ckv_ref[...], kr_ref[...], wkn_ref, wv_ref, gk_ref, None, k_ref, v_ref)


def _mla_weights(w_in, g_q_a, g_kv_a, w_q_b, w_kv_b, g_q, g_k):
    d = w_in.shape[0]
    w_kr = w_in[:, MLA_Q_RANK + MLA_KV_RANK:]
    w_kr = jnp.pad(w_kr, ((0, 0), (MLA_NOPE, LANES - MLA_QK)))
    win = jnp.concatenate([w_in[:, :MLA_Q_RANK + MLA_KV_RANK], w_kr], axis=1).astype(BF16)
    wqb = w_q_b.reshape(MLA_Q_RANK, MLA_HEADS, MLA_QK)
    wqb = jnp.pad(wqb, ((0, 0), (0, 0), (0, LANES - MLA_QK))).reshape(MLA_Q_RANK, MLA_HEADS * LANES)
    wkv = w_kv_b.reshape(MLA_KV_RANK, MLA_HEADS, 2 * MLA_NOPE)
    wkn = jnp.pad(wkv[:, :, :MLA_NOPE], ((0, 0), (0, 0), (0, LANES - MLA_NOPE)))
    wkn = wkn.reshape(MLA_KV_RANK, MLA_HEADS * LANES)
    wv = wkv[:, :, MLA_NOPE:].reshape(MLA_KV_RANK, MLA_HEADS * MLA_NOPE)
    pad = (0, LANES - MLA_QK)
    return dict(win=win, gqa=g_q_a.reshape(1, -1), gkva=g_kv_a.reshape(1, -1),
                wqb=wqb.astype(BF16), gq=jnp.pad(g_q, pad).reshape(1, LANES),
                wkn=wkn.astype(BF16), wv=wv.astype(BF16), gk=jnp.pad(g_k, pad).reshape(1, LANES))


def _rope_tables(tk):
    t = jnp.arange(tk.dec_seq)
    row = (t // GRID_W).astype(F32)
    col = (t % GRID_W).astype(F32)
    half = MLA_ROPE // 2
    inv = 1.0 / (ROPE_BASE ** (jnp.arange(0, half, 2, dtype=F32) / half))
    ar = row[:, None] * inv[None, :]
    ac = col[:, None] * inv[None, :]
    ang = jnp.concatenate([ar, ar, ac, ac], axis=-1)
    cos, sin = jnp.cos(ang), jnp.sin(ang)
    grp = (np.arange(MLA_ROPE) // 8) % 2
    sina = jnp.where(grp == 0, -sin, 0.0)
    sinb = jnp.where(grp == 1, sin, 0.0)

    def place(x, fill):
        full = jnp.full((tk.dec_seq + tk.tm, LANES), fill, F32)
        full = full.at[:tk.dec_seq, MLA_NOPE:MLA_QK].set(x)
        return full.at[tk.dec_seq:, MLA_NOPE:MLA_QK].set(fill)

    return place(cos, 1.0), place(sina, 0.0), place(sinb, 0.0)


def _mla_proj(tk, y, mod, gmix, w, rope_tabs):
    hw = MLA_HEADS * LANES
    rope_spec = pl.BlockSpec((tk.tm, LANES), lambda t: (tk.rope_blk(t), 0))
    names = ("win", "gqa", "gkva", "wqb", "gq", "wkn", "wv", "gk")
    return pl.pallas_call(
        _mla_proj_kernel,
        out_shape=(jax.ShapeDtypeStruct((tk.n, hw), BF16),
                   jax.ShapeDtypeStruct((tk.n, hw), BF16),
                   jax.ShapeDtypeStruct((tk.n, MLA_HEADS * MLA_NOPE), BF16),
                   jax.ShapeDtypeStruct((tk.n, MLA_KV_RANK), F32),
                   jax.ShapeDtypeStruct((tk.n, LANES), F32)),
        grid=(tk.nt,),
        in_specs=[_tok_spec(tk, D_MODEL), _mod_spec(tk), _const_spec((1, D_MODEL))]
        + [_const_spec(w[n].shape) for n in names] + [rope_spec] * 3,
        out_specs=(_tok_spec(tk, hw), _tok_spec(tk, hw), _tok_spec(tk, MLA_HEADS * MLA_NOPE),
                   _tok_spec(tk, MLA_KV_RANK), _tok_spec(tk, LANES)),
        compiler_params=_cparams(("parallel",)),
        name="mla_proj",
    )(y, mod, gmix, *[w[n] for n in names], *rope_tabs)


def _mla_cache_kv(ckv, kr128, w):
    n = ckv.shape[0]
    tm = TOKEN_TILE
    hw = MLA_HEADS * LANES
    spec = lambda width: pl.BlockSpec((tm, width), lambda t: (t, 0))
    return pl.pallas_call(
        _mla_cache_kernel,
        out_shape=(jax.ShapeDtypeStruct((n, hw), BF16),
                   jax.ShapeDtypeStruct((n, MLA_HEADS * MLA_NOPE), BF16)),
        grid=(n // tm,),
        in_specs=[spec(MLA_KV_RANK), spec(LANES)] + [_const_spec(w[k].shape) for k in ("wkn", "wv", "gk")],
        out_specs=(spec(hw), spec(MLA_HEADS * MLA_NOPE)),
        compiler_params=_cparams(("parallel",)),
        name="mla_cache_kv",
    )(ckv, kr128, w["wkn"], w["wv"], w["gk"])


def _flash_kernel(q_ref, k_ref, v_ref, o_ref, m_scr, l_scr, acc_scr, *, scale, width):
    ki = pl.program_id(3)

    @pl.when(ki == 0)
    def _():
        m_scr[...] = jnp.full(m_scr.shape, -jnp.inf, F32)
        l_scr[...] = jnp.zeros(l_scr.shape, F32)
        acc_scr[...] = jnp.zeros(acc_scr.shape, F32)

    q = q_ref[...]
    k = k_ref[...]
    v = v_ref[...]
    lane = lax.broadcasted_iota(jnp.int32, (1, width), 1)
    for e in range(2):
        qe = jnp.where((lane >= e * (width // 2)) & (lane < (e + 1) * (width // 2)), q, jnp.zeros_like(q))
        s = lax.dot_general(qe, k, _NT, preferred_element_type=F32) * scale
        m_prev = m_scr[e]
        m_new = jnp.maximum(m_prev, jnp.max(s, axis=-1, keepdims=True))
        alpha = jnp.exp(m_prev - m_new)
        p = jnp.exp(s - m_new)
        l_scr[e] = alpha * l_scr[e] + jnp.sum(p, axis=-1, keepdims=True)
        acc_scr[e] = alpha * acc_scr[e] + jnp.dot(p.astype(BF16), v, preferred_element_type=F32)
        m_scr[e] = m_new

    @pl.when(ki == pl.num_programs(3) - 1)
    def _():
        lane_o = lax.broadcasted_iota(jnp.int32, (1, LANES), 1)
        o0 = acc_scr[0] / l_scr[0]
        o1 = acc_scr[1] / l_scr[1]
        o_ref[...] = jnp.where(lane_o < LANES // 2, o0, o1).astype(o_ref.dtype)


def _flash(q, k, v, *, nb, tq_len, tk_len, q_off, k_off, width, scale, tq, tkb):
    hp = q.shape[1] // width
    nq, nk = tq_len // tq, tk_len // tkb
    qo, ko = q_off // tq, k_off // tkb
    assert q_off % tq == 0 and k_off % tkb == 0
    return pl.pallas_call(
        functools.partial(_flash_kernel, scale=scale, width=width),
        out_shape=jax.ShapeDtypeStruct((nb * tq_len, hp * LANES), BF16),
        grid=(nb, hp, nq, nk),
        in_specs=[pl.BlockSpec((tq, width), lambda b, h, i, j: (qo + b * nq + i, h)),
                  pl.BlockSpec((tkb, width), lambda b, h, i, j: (ko + b * nk + j, h)),
                  pl.BlockSpec((tkb, LANES), lambda b, h, i, j: (ko + b * nk + j, h))],
        out_specs=pl.BlockSpec((tq, LANES), lambda b, h, i, j: (b * nq + i, h)),
        scratch_shapes=[pltpu.VMEM((2, tq, 1), F32), pltpu.VMEM((2, tq, 1), F32),
                        pltpu.VMEM((2, tq, LANES), F32)],
        compiler_params=_cparams(("parallel", "parallel", "parallel", "arbitrary")),
        name="flash_attn",
    )(q, k, v)


def _na_proj_kernel(y_ref, mod_ref, gmix_ref, win_ref, gq_ref, gk_ref,
                    q_ref, k_ref, v_ref, kf_ref, vf_ref):
    h = _modulate(y_ref[...], gmix_ref[...], mod_ref[0:1, :], mod_ref[1:2, :])
    qkv = jnp.dot(h.astype(BF16), win_ref[...], preferred_element_type=F32)
    d = D_MODEL
    even = lax.broadcasted_iota(jnp.int32, (1, LANES), 1) < NA_HD

    def pair_norm(x, g):
        x2 = x * x
        se = jnp.sum(jnp.where(even, x2, 0.0), axis=-1, keepdims=True)
        so = jnp.sum(jnp.where(even, 0.0, x2), axis=-1, keepdims=True)
        r = jnp.where(even, lax.rsqrt(se * (1.0 / NA_HD) + EPS), lax.rsqrt(so * (1.0 / NA_HD) + EPS))
        return x * r * g

    gq, gk = gq_ref[...], gk_ref[...]
    for blk in range(d // LANES):
        sl = slice(blk * LANES, (blk + 1) * LANES)
        q_ref[:, sl] = pair_norm(qkv[:, sl], gq).astype(BF16)
        kn = pair_norm(qkv[:, d + blk * LANES:d + (blk + 1) * LANES], gk)
        k_ref[:, sl] = kn.astype(BF16)
        kf_ref[:, sl] = kn
    vv = qkv[:, 2 * d:]
    v_ref[...] = vv.astype(BF16)
    vf_ref[...] = vv


def _na_proj(tk, y, mod, gmix, win, gq2, gk2):
    d = D_MODEL
    ctx_spec = pl.BlockSpec((tk.tm, d), lambda t: (jnp.minimum(t, tk.nct), 0))
    return pl.pallas_call(
        _na_proj_kernel,
        out_shape=(jax.ShapeDtypeStruct((tk.n, d), BF16),) * 3
        + (jax.ShapeDtypeStruct((tk.nc + tk.tm, d), F32),) * 2,
        grid=(tk.nt,),
        in_specs=[_tok_spec(tk, d), _mod_spec(tk), _const_spec((1, d)), _const_spec(win.shape),
                  _const_spec((1, LANES)), _const_spec((1, LANES))],
        out_specs=(_tok_spec(tk, d),) * 3 + (ctx_spec,) * 2,
        compiler_params=_cparams(("arbitrary",)),
        name="na_proj",
    )(y, mod, gmix, win, gq2, gk2)


def _na_bias_tables(rel_bias, rows):
    ngroups = rows // NA_GROUP_ROWS
    wr = min(NA_WIN_R, rows)
    assert rows >= NA_SLAB_ROWS and wr == NA_WIN_R

    def geometry(g):
        base = int(np.clip(NA_GROUP_ROWS * g - NA_WIN_R // 2, 0, rows - NA_SLAB_ROWS))
        r = NA_GROUP_ROWS * g + np.arange(NA_GROUP_ROWS)[:, None, None, None]
        c = np.arange(GRID_W)[None, :, None, None]
        kr = base + np.arange(NA_SLAB_ROWS)[None, None, :, None]
        kc = np.arange(GRID_W)[None, None, None, :]
        rs = np.clip(r - wr // 2, 0, rows - wr)
        cs = np.clip(c - NA_WIN_C // 2, 0, GRID_W - NA_WIN_C)
        valid = (kr >= rs) & (kr < rs + wr) & (kc >= cs) & (kc < cs + NA_WIN_C)
        dr = np.clip(kr - r + (NA_WIN_R - 1), 0, 2 * NA_WIN_R - 2)
        dc = np.clip(kc - c + (NA_WIN_C - 1), 0, 2 * NA_WIN_C - 2)
        shape = (NA_GROUP_ROWS * GRID_W, NA_SLAB_ROWS * GRID_W)
        full = np.broadcast_to
        s4 = valid.shape
        return (full(valid, s4).reshape(shape), full(dr, s4).reshape(shape), full(dc, s4).reshape(shape))

    pats = [geometry(0), geometry(1), geometry(ngroups - 1)]
    for g in range(2, ngroups - 1):
        valid, dr, dc = geometry(g)
        assert np.array_equal(valid, pats[1][0])
        assert np.array_equal(dr[valid], pats[1][1][valid]) and np.array_equal(dc[valid], pats[1][2][valid])
    tabs = []
    for valid, dr, dc in pats:
        tabs.append(jnp.where(valid[None], rel_bias[:, dr, dc], NEG_BIG))
    return jnp.stack(tabs, axis=1)


def _na_lat_kernel(q_ref, k_ref, v_ref, kc_ref, vc_ref, b_ref, o_ref, *, scale, rows):
    g = pl.program_id(2)
    base = jnp.clip(NA_GROUP_ROWS * g - NA_WIN_R // 2, 0, rows - NA_SLAB_ROWS)
    start = pl.multiple_of(base * GRID_W, GRID_W)
    nkeys = NA_SLAB_ROWS * GRID_W
    kw = k_ref[pl.ds(start, nkeys), :]
    vw = v_ref[pl.ds(start, nkeys), :]
    kc = kc_ref[...]
    vc = vc_ref[...]
    q = q_ref[...]
    lane = lax.broadcasted_iota(jnp.int32, (1, LANES), 1)
    outs = []
    for e in range(2):
        mine = (lane < NA_HD) if e == 0 else (lane >= NA_HD)
        qe = jnp.where(mine, q, jnp.zeros_like(q))
        sc = lax.dot_general(qe, kc, _NT, preferred_element_type=F32) * scale
        sw = lax.dot_general(qe, kw, _NT, preferred_element_type=F32) * scale + b_ref[e]
        m = jnp.maximum(jnp.max(sc, axis=-1, keepdims=True), jnp.max(sw, axis=-1, keepdims=True))
        pc = jnp.exp(sc - m)
        pw = jnp.exp(sw - m)
        l = jnp.sum(pc, axis=-1, keepdims=True) + jnp.sum(pw, axis=-1, keepdims=True)
        o = (jnp.dot(pc.astype(BF16), vc, preferred_element_type=F32)
             + jnp.dot(pw.astype(BF16), vw, preferred_element_type=F32))
        outs.append(o / l)
    o_ref[...] = jnp.where(lane < NA_HD, outs[0], outs[1]).astype(o_ref.dtype)


def _na_latent(tk, q, k, v, kc, vc, bias):
    rows = tk.dec_seq // GRID_W
    ngroups = rows // NA_GROUP_ROWS
    gq = NA_GROUP_ROWS * GRID_W
    nkeys = NA_SLAB_ROWS * GRID_W
    hp = NA_HEADS // 2
    past = kc.shape[2]
    qo = tk.nc // gq
    ko = tk.nc // tk.dec_seq
    assert tk.nc % tk.dec_seq == 0

    def pat(g):
        return jnp.where(g == 0, 0, jnp.where(g == ngroups - 1, 2, 1))

    return pl.pallas_call(
        functools.partial(_na_lat_kernel, scale=NA_HD ** -0.5, rows=rows),
        out_shape=jax.ShapeDtypeStruct((tk.nl, D_MODEL), BF16),
        grid=(tk.dec_batch, hp, ngroups),
        in_specs=[pl.BlockSpec((gq, LANES), lambda b, h, g: (qo + b * ngroups + g, h)),
                  pl.BlockSpec((tk.dec_seq, LANES), lambda b, h, g: (ko + b, h)),
                  pl.BlockSpec((tk.dec_seq, LANES), lambda b, h, g: (ko + b, h)),
                  pl.BlockSpec((None, None, past, LANES), lambda b, h, g: (b, h, 0, 0)),
                  pl.BlockSpec((None, None, past, LANES), lambda b, h, g: (b, h, 0, 0)),
                  pl.BlockSpec((2, None, gq, nkeys), lambda b, h, g: (h, pat(g), 0, 0))],
        out_specs=pl.BlockSpec((gq, LANES), lambda b, h, g: (b * ngroups + g, h)),
        compiler_params=_cparams(("parallel", "parallel", "arbitrary")),
        name="na_latent",
    )(q, k, v, kc, vc, bias)


def _out_proj_kernel(y_ref, o_ref, w_ref, mod_ref, out_ref):
    out_ref[...] = y_ref[...] + mod_ref[2:3, :] * jnp.dot(o_ref[...], w_ref[...], preferred_element_type=F32)


def _out_proj(tk, y, o, w_o, mod):
    d = D_MODEL
    return pl.pallas_call(
        _out_proj_kernel,
        out_shape=jax.ShapeDtypeStruct((tk.n, d), F32),
        grid=(tk.nt,),
        in_specs=[_tok_spec(tk, d), _tok_spec(tk, d), _const_spec((d, d)), _mod_spec(tk)],
        out_specs=_tok_spec(tk, d),
        compiler_params=_cparams(("parallel",)),
        name="out_proj",
    )(y, o, w_o, mod)


def _peer_pairs():
    return [(k, l) for k in range(1, PEER_NTOP + 1) for l in range(1, PEER_NTOP // k + 1)]


def _extract_sorted(x, n):
    out = []
    for _ in range(n):
        m = jnp.max(x, axis=0, keepdims=True)
        out.append(m)
        x = jnp.where(x >= m, -jnp.inf, x)
    return out


def _peer_router_kernel(y_ref, mod_ref, gffn_ref, wqh_ref, wql_ref, skh_ref, skl_ref,
                        h_ref, th_ref, c_ref, s1_ref, e1_ref, s_scr, top_scr, cand_scr):
    h = _modulate(y_ref[...], gffn_ref[...], mod_ref[3:4, :], mod_ref[4:5, :])
    hh, hl = _split_bf16(h)
    h_ref[...] = hh
    wqh = wqh_ref[...]
    q = (jnp.dot(hh, wqh, preferred_element_type=F32) + jnp.dot(hh, wql_ref[...], preferred_element_type=F32)
         + jnp.dot(hl, wqh, preferred_element_type=F32))
    nlist = 2 * PEER_HEADS
    for n in range(nlist):
        qh, ql = _split_bf16(q[:, n * LANES:(n + 1) * LANES])
        kh = skh_ref[n]
        s_scr[n] = (lax.dot_general(kh, qh, _NT, preferred_element_type=F32)
                    + lax.dot_general(skl_ref[n], qh, _NT, preferred_element_type=F32)
                    + lax.dot_general(kh, ql, _NT, preferred_element_type=F32))

    def extract(n, carry):
        for r, m in enumerate(_extract_sorted(s_scr[n], PEER_NTOP)):
            top_scr[n, r:r + 1, :] = m
        return carry

    lax.fori_loop(0, nlist, extract, 0)

    pairs = _peer_pairs()
    npad = cand_scr.shape[0]

    def head(hd, carry):
        a = top_scr[2 * hd]
        b = top_scr[2 * hd + 1]
        for r, (k, l) in enumerate(pairs):
            cand_scr[r:r + 1, :] = a[k - 1:k, :] + b[l - 1:l, :]
        cand_scr[len(pairs):npad, :] = jnp.full((npad - len(pairs), cand_scr.shape[1]), -jnp.inf, F32)
        ts = _extract_sorted(cand_scr[...], PEER_NTOP)
        z = jnp.zeros_like(ts[0])
        for t in ts[:PEER_TOPK]:
            z = z + jnp.exp(t - ts[0])
        tau = 0.5 * (ts[PEER_TOPK - 1] + ts[PEER_TOPK])
        s0 = s_scr[2 * hd]
        s1 = s_scr[2 * hd + 1]
        th_ref[hd] = tau - s0
        c_ref[hd] = jnp.exp(s0 - a[0:1, :]) * (1.0 / z)
        s1_ref[hd] = s1
        e1_ref[hd] = jnp.exp(s1 - b[0:1, :])
        return carry

    lax.fori_loop(0, PEER_HEADS, head, 0)


def _peer_router(tk, y, mod, gffn, wqh, wql, skh, skl):
    d = D_MODEL
    nlist = 2 * PEER_HEADS
    npairs = len(_peer_pairs())
    rspec = pl.BlockSpec((PEER_HEADS, PEER_KEYS, tk.tm), lambda t: (0, 0, t))
    rshape = jax.ShapeDtypeStruct((PEER_HEADS, PEER_KEYS, tk.n), F32)
    return pl.pallas_call(
        _peer_router_kernel,
        out_shape=(jax.ShapeDtypeStruct((tk.n, d), BF16), rshape, rshape, rshape, rshape),
        grid=(tk.nt,),
        in_specs=[_tok_spec(tk, d), _mod_spec(tk), _const_spec((1, d)),
                  _const_spec(wqh.shape), _const_spec(wql.shape),
                  _const_spec(skh.shape), _const_spec(skl.shape)],
        out_specs=(_tok_spec(tk, d), rspec, rspec, rspec, rspec),
        scratch_shapes=[pltpu.VMEM((nlist, PEER_KEYS, tk.tm), F32),
                        pltpu.VMEM((nlist, 24, tk.tm), F32),
                        pltpu.VMEM((-(-npairs // 8) * 8, tk.tm), F32)],
        compiler_params=_cparams(("parallel",)),
        name="peer_router",
    )(y, mod, gffn, wqh, wql, skh, skl)


def _peer_dense_kernel(y_ref, mod_ref, h_ref, th_ref, c_ref, s1_ref, e1_ref, u_ref, vt_ref,
                       out_ref, s_scr, p_scr, acc_scr):
    eb = pl.program_id(1)

    @pl.when(eb == 0)
    def _():
        acc_scr[...] = jnp.zeros(acc_scr.shape, F32)

    s_scr[...] = lax.dot_general(u_ref[...], h_ref[...], _NT, preferred_element_type=F32)
    rows_per_block = s_scr.shape[0] // PEER_KEYS

    def row(il, carry):
        i = eb * rows_per_block + il
        r0 = pl.multiple_of(il * PEER_KEYS, PEER_KEYS)
        s = s_scr[pl.ds(r0, PEER_KEYS), :]
        act = s * (lax.erf(s * (2.0 ** -0.5)) + 1.0) * 0.5
        w = jnp.zeros_like(s)
        for hd in range(PEER_HEADS):
            th = th_ref[hd, pl.ds(i, 1), :]
            cc = c_ref[hd, pl.ds(i, 1), :]
            w = w + jnp.where(s1_ref[hd] >= th, e1_ref[hd], 0.0) * cc
        p_scr[pl.ds(r0, PEER_KEYS), :] = (w * act).astype(BF16)
        return carry

    lax.fori_loop(0, rows_per_block, row, 0)
    acc_scr[...] += jnp.dot(vt_ref[...], p_scr[...], preferred_element_type=F32)

    @pl.when(eb == pl.num_programs(1) - 1)
    def _():
        out_ref[...] = y_ref[...] + mod_ref[5:6, :] * acc_scr[...].T


def _peer_dense(tk, y, mod, h, th, c, s1, e1, u_bf, vt_bf):
    d = D_MODEL
    nexp = u_bf.shape[0]
    eb = PEER_EXPERT_BLOCK
    tok = lambda width: pl.BlockSpec((tk.tm, width), lambda t, e: (t, 0))
    rspec = pl.BlockSpec((PEER_HEADS, PEER_KEYS, tk.tm), lambda t, e: (0, 0, t))
    return pl.pallas_call(
        _peer_dense_kernel,
        out_shape=jax.ShapeDtypeStruct((tk.n, d), F32),
        grid=(tk.nt, nexp // eb),
        in_specs=[tok(d), pl.BlockSpec((None, 6, d), lambda t, e: (tk.cond_row(t), 0, 0)), tok(d),
                  rspec, rspec, rspec, rspec,
                  pl.BlockSpec((eb, d), lambda t, e: (e, 0)),
                  pl.BlockSpec((d, eb), lambda t, e: (0, e))],
        out_specs=tok(d),
        scratch_shapes=[pltpu.VMEM((eb, tk.tm), F32), pltpu.VMEM((eb, tk.tm), BF16),
                        pltpu.VMEM((d, tk.tm), F32)],
        compiler_params=_cparams(("parallel", "arbitrary")),
        name="peer_dense",
    )(y, mod, h, th, c, s1, e1, u_bf, vt_bf)


def kernel(x_prompt, x_sample, cache_mla_ckv, cache_mla_krope, cache_na_k, cache_na_v, c, c_ctx, w_mod, b_mod, g_norm_mix, g_norm_ffn, mla_w_in, mla_g_q_a, mla_g_kv_a, mla_w_q_b, mla_w_kv_b, mla_g_q, mla_g_k, mla_w_o, na_w_in, na_g_q, na_g_k, na_rel_bias, na_w_o, peer_w_q, peer_sub_keys, peer_u, peer_v):
    batch, seq, d = x_prompt.shape
    dec_batch, dec_seq, _ = x_sample.shape
    past = cache_mla_ckv.shape[2]
    depth = w_mod.shape[0]
    tk = _Tokens(batch, seq, dec_batch, dec_seq)
    assert dec_batch + 1 <= 8 and d == D_MODEL

    y = jnp.concatenate([x_prompt.reshape(tk.nc, d), x_sample.reshape(tk.nl, d)], axis=0)
    cond8 = jnp.zeros((8, d), F32).at[:dec_batch].set(c).at[dec_batch].set(c_ctx)
    mod_all = _adaln(cond8, w_mod, b_mod).reshape(depth, 8, 6, d)
    rope_tabs = _rope_tables(tk)
    rows = dec_seq // GRID_W

    new_ckv, new_krope, new_k, new_v = [], [], [], []
    for i in range(depth):
        j = i // 2
        mod = mod_all[i]
        gmix = g_norm_mix[i].reshape(1, d)
        if i % 2 == 0:
            w = _mla_weights(mla_w_in[j], mla_g_q_a[j], mla_g_kv_a[j], mla_w_q_b[j], mla_w_kv_b[j],
                             mla_g_q[j], mla_g_k[j])
            q, k, v, ckv_n, kr = _mla_proj(tk, y, mod, gmix, w, rope_tabs)
            new_ckv.append(ckv_n[:tk.nc].reshape(batch, seq, MLA_KV_RANK))
            new_krope.append(kr[:tk.nc, MLA_NOPE:MLA_QK].reshape(batch, seq, MLA_ROPE))
            kr_c = jnp.pad(cache_mla_krope[:, j], ((0, 0), (0, 0), (MLA_NOPE, LANES - MLA_QK)))
            kc, vc = _mla_cache_kv(cache_mla_ckv[:, j].reshape(dec_batch * past, MLA_KV_RANK),
                                   kr_c.reshape(dec_batch * past, LANES), w)
            scale = MLA_QK ** -0.5
            o_ctx = _flash(q, k, v, nb=batch, tq_len=seq, tk_len=seq, q_off=0, k_off=0,
                           width=2 * LANES, scale=scale, tq=seq, tkb=seq)
            kfull = jnp.concatenate([kc.reshape(dec_batch, past, -1),
                                     k[tk.nc:].reshape(dec_batch, dec_seq, -1)], axis=1)
            vfull = jnp.concatenate([vc.reshape(dec_batch, past, -1),
                                     v[tk.nc:].reshape(dec_batch, dec_seq, -1)], axis=1)
            klen = past + dec_seq
            o_lat = _flash(q, kfull.reshape(dec_batch * klen, -1), vfull.reshape(dec_batch * klen, -1),
                           nb=dec_batch, tq_len=dec_seq, tk_len=klen, q_off=tk.nc, k_off=0,
                           width=2 * LANES, scale=scale, tq=512, tkb=512)
            w_o = mla_w_o[j].astype(BF16)
        else:
            gq2 = jnp.tile(na_g_q[j], 2).reshape(1, LANES)
            gk2 = jnp.tile(na_g_k[j], 2).reshape(1, LANES)
            q, k, v, kf, vf = _na_proj(tk, y, mod, gmix, na_w_in[j].astype(BF16), gq2, gk2)
            new_k.append(kf[:tk.nc].reshape(batch, seq, NA_HEADS, NA_HD).transpose(0, 2, 1, 3))
            new_v.append(vf[:tk.nc].reshape(batch, seq, NA_HEADS, NA_HD).transpose(0, 2, 1, 3))
            scale = NA_HD ** -0.5
            o_ctx = _flash(q, k, v, nb=batch, tq_len=seq, tk_len=seq, q_off=0, k_off=0,
                           width=LANES, scale=scale, tq=seq, tkb=seq)

            def pair_layout(x):
                b_, h_, l_, e_ = x.shape
                return x.reshape(b_, h_ // 2, 2, l_, e_).transpose(0, 1, 3, 2, 4).reshape(b_, h_ // 2, l_, 2 * e_)

            kc = pair_layout(cache_na_k[:, j]).astype(BF16)
            vc = pair_layout(cache_na_v[:, j]).astype(BF16)
            bias = _na_bias_tables(na_rel_bias[j], rows)
            o_lat = _na_latent(tk, q, k, v, kc, vc, bias)
            w_o = na_w_o[j].astype(BF16)
        o = jnp.concatenate([o_ctx, o_lat], axis=0)
        y = _out_proj(tk, y, o, w_o, mod)

        wqh, wql = _split_bf16(peer_w_q[i])
        skh, skl = _split_bf16(peer_sub_keys[i].reshape(2 * PEER_HEADS, PEER_KEYS, PEER_KEYS))
        h, th, cc, s1, e1 = _peer_router(tk, y, mod, g_norm_ffn[i].reshape(1, d), wqh, wql, skh, skl)
        u_bf = peer_u[i].astype(BF16)
        vt_bf = peer_v[i].T.astype(BF16)
        y = _peer_dense(tk, y, mod, h, th, cc, s1, e1, u_bf, vt_bf)

    return (y[:tk.nc].reshape(batch, seq, d), y[tk.nc:].reshape(dec_batch, dec_seq, d),
            jnp.stack(new_ckv, axis=1), jnp.stack(new_krope, axis=1),
            jnp.stack(new_k, axis=1), jnp.stack(new_v, axis=1))
```

```python
import functools

import numpy as np
import jax
import jax.numpy as jnp
from jax import lax
from jax.experimental import pallas as pl
from jax.experimental.pallas import tpu as pltpu

F32 = jnp.float32
BF16 = jnp.bfloat16

D_MODEL = 1024
GRID_W = 64
MLA_HEADS = 16
MLA_NOPE = 64
MLA_ROPE = 32
MLA_QK = MLA_NOPE + MLA_ROPE
MLA_Q_RANK = 256
MLA_KV_RANK = 128
NA_HEADS = 16
NA_HD = 64
NA_WIN_R = 8
NA_WIN_C = 16
PEER_HEADS = 8
PEER_KEYS = 128
PEER_TOPK = 16
ROPE_BASE = 10000.0
EPS = 1e-6

LANES = 128
TOKEN_TILE = 256
NA_GROUP_ROWS = 4
NA_SLAB_ROWS = NA_WIN_R + NA_GROUP_ROWS
PEER_EXPERT_BLOCK = 1024
PEER_TOKEN_TILE = 512
PEER_LANE_CHUNK = 256
PEER_NTOP = PEER_TOPK + 1
NEG_BIG = -1e30
VMEM_LIMIT = 56 * 1024 * 1024

_NT = (((1,), (1,)), ((), ()))


def _cparams(sem):
    return pltpu.CompilerParams(dimension_semantics=sem, vmem_limit_bytes=VMEM_LIMIT)


def _modulate(y, g, shift, scale):
    ms = jnp.mean(y * y, axis=-1, keepdims=True)
    return (y * lax.rsqrt(ms + EPS)) * g * (1.0 + scale) + shift


def _rms(x, n):
    return x * lax.rsqrt(jnp.sum(x * x, axis=-1, keepdims=True) * (1.0 / n) + EPS)


def _split_bf16(x):
    hi = x.astype(BF16)
    lo = (x - hi.astype(F32)).astype(BF16)
    return hi, lo


def _adaln_kernel(c_ref, w_ref, b_ref, o_ref):
    c = c_ref[...]
    s = c / (1.0 + jnp.exp(-c))
    o_ref[...] = jnp.dot(s, w_ref[...], preferred_element_type=F32,
                         precision=lax.Precision.HIGHEST) + b_ref[...]


def _adaln(cond8, w_mod, b_mod):
    depth, d, d6 = w_mod.shape
    nj = d6 // d
    return pl.pallas_call(
        _adaln_kernel,
        out_shape=jax.ShapeDtypeStruct((depth, 8, d6), F32),
        grid=(depth, nj),
        in_specs=[pl.BlockSpec((8, d), lambda l, j: (0, 0)),
                  pl.BlockSpec((None, d, d), lambda l, j: (l, 0, j)),
                  pl.BlockSpec((None, 1, d), lambda l, j: (l, 0, j))],
        out_specs=pl.BlockSpec((None, 8, d), lambda l, j: (l, 0, j)),
        compiler_params=_cparams(("parallel", "parallel")),
        name="adaln",
    )(cond8, w_mod, b_mod.reshape(depth, 1, d6))


class _Tokens:
    def __init__(self, batch, seq, dec_batch, dec_seq):
        self.batch, self.seq, self.dec_batch, self.dec_seq = batch, seq, dec_batch, dec_seq
        self.nc = batch * seq
        self.nl = dec_batch * dec_seq
        self.n = self.nc + self.nl
        tm = TOKEN_TILE
        assert seq % tm == 0 and dec_seq % tm == 0
        self.tm = tm
        self.nct = self.nc // tm
        self.tpb = dec_seq // tm
        self.nt = self.n // tm
        self.ctx_row = dec_batch

    def cond_row(self, t, tm=None):
        tm = self.tm if tm is None else tm
        nct, tpb = self.nc // tm, self.dec_seq // tm
        lat = lax.div(jnp.maximum(t - nct, 0), tpb)
        return jnp.where(t < nct, self.ctx_row, lat)

    def rope_blk(self, t):
        lat = lax.rem(jnp.maximum(t - self.nct, 0), self.tpb)
        return jnp.where(t < self.nct, self.tpb, lat)


def _tok_spec(tk, width):
    return pl.BlockSpec((tk.tm, width), lambda t: (t, 0))


def _mod_spec(tk):
    return pl.BlockSpec((None, 6, D_MODEL), lambda t: (tk.cond_row(t), 0, 0))


def _const_spec(shape):
    nd = len(shape)
    return pl.BlockSpec(shape, lambda t: (0,) * nd)


def _rope(x, cos, sina, sinb):
    return x * cos + pltpu.roll(x, LANES - 8, 1) * sina + pltpu.roll(x, 8, 1) * sinb


def _mla_kv_tail(ckv_n, kr, wkn_ref, wv_ref, gk_ref, rope, k_ref, v_ref):
    cb = ckv_n.astype(BF16)
    kn = jnp.dot(cb, wkn_ref[...], preferred_element_type=F32)
    v_ref[...] = jnp.dot(cb, wv_ref[...], preferred_element_type=F32).astype(BF16)
    gk = gk_ref[...]
    for hd in range(MLA_HEADS):
        sl = slice(hd * LANES, (hd + 1) * LANES)
        kh = _rms(kn[:, sl] + kr, MLA_QK) * gk
        if rope is not None:
            kh = _rope(kh, *rope)
        k_ref[:, sl] = kh.astype(BF16)


def _mla_proj_kernel(y_ref, mod_ref, gmix_ref, win_ref, gqa_ref, gkva_ref, wqb_ref, gq_ref,
                     wkn_ref, wv_ref, gk_ref, cos_ref, sina_ref, sinb_ref,
                     q_ref, k_ref, v_ref, ckv_ref, kr_ref):
    h = _modulate(y_ref[...], gmix_ref[...], mod_ref[0:1, :], mod_ref[1:2, :])
    a = jnp.dot(h.astype(BF16), win_ref[...], preferred_element_type=F32)
    qa = a[:, :MLA_Q_RANK]
    ckv = a[:, MLA_Q_RANK:MLA_Q_RANK + MLA_KV_RANK]
    kr = a[:, MLA_Q_RANK + MLA_KV_RANK:]
    qn = _rms(qa, MLA_Q_RANK) * gqa_ref[...]
    ckv_n = _rms(ckv, MLA_KV_RANK) * gkva_ref[...]
    ckv_ref[...] = ckv_n
    kr_ref[...] = kr
    rope = (cos_ref[...], sina_ref[...], sinb_ref[...])
    q = jnp.dot(qn.astype(BF16), wqb_ref[...], preferred_element_type=F32)
    gq = gq_ref[...]
    for hd in range(MLA_HEADS):
        sl = slice(hd * LANES, (hd + 1) * LANES)
        qh = _rope(_rms(q[:, sl], MLA_QK) * gq, *rope)
        q_ref[:, sl] = qh.astype(BF16)
    _mla_kv_tail(ckv_n, kr, wkn_ref, wv_ref, gk_ref, rope, k_ref, v_ref)


def _mla_cache_kernel(ckv_ref, kr_ref, wkn_ref, wv_ref, gk_ref, k_ref, v_ref):
    _mla_kv_tail(ckv_ref[...], kr_ref[...], wkn_ref, wv_ref, gk_ref, None, k_ref, v_ref)


def _mla_weights(w_in, g_q_a, g_kv_a, w_q_b, w_kv_b, g_q, g_k):
    d = w_in.shape[0]
    w_kr = w_in[:, MLA_Q_RANK + MLA_KV_RANK:]
    w_kr = jnp.pad(w_kr, ((0, 0), (MLA_NOPE, LANES - MLA_QK)))
    win = jnp.concatenate([w_in[:, :MLA_Q_RANK + MLA_KV_RANK], w_kr], axis=1).astype(BF16)
    wqb = w_q_b.reshape(MLA_Q_RANK, MLA_HEADS, MLA_QK)
    wqb = jnp.pad(wqb, ((0, 0), (0, 0), (0, LANES - MLA_QK))).reshape(MLA_Q_RANK, MLA_HEADS * LANES)
    wkv = w_kv_b.reshape(MLA_KV_RANK, MLA_HEADS, 2 * MLA_NOPE)
    wkn = jnp.pad(wkv[:, :, :MLA_NOPE], ((0, 0), (0, 0), (0, LANES - MLA_NOPE)))
    wkn = wkn.reshape(MLA_KV_RANK, MLA_HEADS * LANES)
    wv = wkv[:, :, MLA_NOPE:].reshape(MLA_KV_RANK, MLA_HEADS * MLA_NOPE)
    pad = (0, LANES - MLA_QK)
    return dict(win=win, gqa=g_q_a.reshape(1, -1), gkva=g_kv_a.reshape(1, -1),
                wqb=wqb.astype(BF16), gq=jnp.pad(g_q, pad).reshape(1, LANES),
                wkn=wkn.astype(BF16), wv=wv.astype(BF16), gk=jnp.pad(g_k, pad).reshape(1, LANES))


def _rope_tables(tk):
    t = jnp.arange(tk.dec_seq)
    row = (t // GRID_W).astype(F32)
    col = (t % GRID_W).astype(F32)
    half = MLA_ROPE // 2
    inv = 1.0 / (ROPE_BASE ** (jnp.arange(0, half, 2, dtype=F32) / half))
    ar = row[:, None] * inv[None, :]
    ac = col[:, None] * inv[None, :]
    ang = jnp.concatenate([ar, ar, ac, ac], axis=-1)
    cos, sin = jnp.cos(ang), jnp.sin(ang)
    grp = (np.arange(MLA_ROPE) // 8) % 2
    sina = jnp.where(grp == 0, -sin, 0.0)
    sinb = jnp.where(grp == 1, sin, 0.0)

    def place(x, fill):
        full = jnp.full((tk.dec_seq + tk.tm, LANES), fill, F32)
        full = full.at[:tk.dec_seq, MLA_NOPE:MLA_QK].set(x)
        return full.at[tk.dec_seq:, MLA_NOPE:MLA_QK].set(fill)

    return place(cos, 1.0), place(sina, 0.0), place(sinb, 0.0)


def _mla_proj(tk, y, mod, gmix, w, rope_tabs):
    hw = MLA_HEADS * LANES
    rope_spec = pl.BlockSpec((tk.tm, LANES), lambda t: (tk.rope_blk(t), 0))
    names = ("win", "gqa", "gkva", "wqb", "gq", "wkn", "wv", "gk")
    return pl.pallas_call(
        _mla_proj_kernel,
        out_shape=(jax.ShapeDtypeStruct((tk.n, hw), BF16),
                   jax.ShapeDtypeStruct((tk.n, hw), BF16),
                   jax.ShapeDtypeStruct((tk.n, MLA_HEADS * MLA_NOPE), BF16),
                   jax.ShapeDtypeStruct((tk.n, MLA_KV_RANK), F32),
                   jax.ShapeDtypeStruct((tk.n, LANES), F32)),
        grid=(tk.nt,),
        in_specs=[_tok_spec(tk, D_MODEL), _mod_spec(tk), _const_spec((1, D_MODEL))]
        + [_const_spec(w[n].shape) for n in names] + [rope_spec] * 3,
        out_specs=(_tok_spec(tk, hw), _tok_spec(tk, hw), _tok_spec(tk, MLA_HEADS * MLA_NOPE),
                   _tok_spec(tk, MLA_KV_RANK), _tok_spec(tk, LANES)),
        compiler_params=_cparams(("parallel",)),
        name="mla_proj",
    )(y, mod, gmix, *[w[n] for n in names], *rope_tabs)


def _mla_cache_kv(ckv, kr128, w):
    n = ckv.shape[0]
    tm = TOKEN_TILE
    hw = MLA_HEADS * LANES
    spec = lambda width: pl.BlockSpec((tm, width), lambda t: (t, 0))
    return pl.pallas_call(
        _mla_cache_kernel,
        out_shape=(jax.ShapeDtypeStruct((n, hw), BF16),
                   jax.ShapeDtypeStruct((n, MLA_HEADS * MLA_NOPE), BF16)),
        grid=(n // tm,),
        in_specs=[spec(MLA_KV_RANK), spec(LANES)] + [_const_spec(w[k].shape) for k in ("wkn", "wv", "gk")],
        out_specs=(spec(hw), spec(MLA_HEADS * MLA_NOPE)),
        compiler_params=_cparams(("parallel",)),
        name="mla_cache_kv",
    )(ckv, kr128, w["wkn"], w["wv"], w["gk"])


def _flash_kernel(q_ref, k_ref, v_ref, o_ref, m_scr, l_scr, acc_scr, *, scale, width):
    ki = pl.program_id(3)

    @pl.when(ki == 0)
    def _():
        m_scr[...] = jnp.full(m_scr.shape, -jnp.inf, F32)
        l_scr[...] = jnp.zeros(l_scr.shape, F32)
        acc_scr[...] = jnp.zeros(acc_scr.shape, F32)

    q = q_ref[...]
    k = k_ref[...]
    v = v_ref[...]
    lane = lax.broadcasted_iota(jnp.int32, (1, width), 1)
    for e in range(2):
        qe = jnp.where((lane >= e * (width // 2)) & (lane < (e + 1) * (width // 2)), q, jnp.zeros_like(q))
        s = lax.dot_general(qe, k, _NT, preferred_element_type=F32) * scale
        m_prev = m_scr[e]
        m_new = jnp.maximum(m_prev, jnp.max(s, axis=-1, keepdims=True))
        alpha = jnp.exp(m_prev - m_new)
        p = jnp.exp(s - m_new)
        l_scr[e] = alpha * l_scr[e] + jnp.sum(p, axis=-1, keepdims=True)
        acc_scr[e] = alpha * acc_scr[e] + jnp.dot(p.astype(BF16), v, preferred_element_type=F32)
        m_scr[e] = m_new

    @pl.when(ki == pl.num_programs(3) - 1)
    def _():
        lane_o = lax.broadcasted_iota(jnp.int32, (1, LANES), 1)
        o0 = acc_scr[0] / l_scr[0]
        o1 = acc_scr[1] / l_scr[1]
        o_ref[...] = jnp.where(lane_o < LANES // 2, o0, o1).astype(o_ref.dtype)


def _flash(q, k, v, *, nb, tq_len, tk_len, q_off, k_off, width, scale, tq, tkb):
    hp = q.shape[1] // width
    nq, nk = tq_len // tq, tk_len // tkb
    qo, ko = q_off // tq, k_off // tkb
    assert q_off % tq == 0 and k_off % tkb == 0
    return pl.pallas_call(
        functools.partial(_flash_kernel, scale=scale, width=width),
        out_shape=jax.ShapeDtypeStruct((nb * tq_len, hp * LANES), BF16),
        grid=(nb, hp, nq, nk),
        in_specs=[pl.BlockSpec((tq, width), lambda b, h, i, j: (qo + b * nq + i, h)),
                  pl.BlockSpec((tkb, width), lambda b, h, i, j: (ko + b * nk + j, h)),
                  pl.BlockSpec((tkb, LANES), lambda b, h, i, j: (ko + b * nk + j, h))],
        out_specs=pl.BlockSpec((tq, LANES), lambda b, h, i, j: (b * nq + i, h)),
        scratch_shapes=[pltpu.VMEM((2, tq, 1), F32), pltpu.VMEM((2, tq, 1), F32),
                        pltpu.VMEM((2, tq, LANES), F32)],
        compiler_params=_cparams(("parallel", "parallel", "parallel", "arbitrary")),
        name="flash_attn",
    )(q, k, v)


def _na_proj_kernel(y_ref, mod_ref, gmix_ref, win_ref, gq_ref, gk_ref,
                    q_ref, k_ref, v_ref, kf_ref, vf_ref):
    h = _modulate(y_ref[...], gmix_ref[...], mod_ref[0:1, :], mod_ref[1:2, :])
    qkv = jnp.dot(h.astype(BF16), win_ref[...], preferred_element_type=F32)
    d = D_MODEL
    even = lax.broadcasted_iota(jnp.int32, (1, LANES), 1) < NA_HD

    def pair_norm(x, g):
        x2 = x * x
        se = jnp.sum(jnp.where(even, x2, 0.0), axis=-1, keepdims=True)
        so = jnp.sum(jnp.where(even, 0.0, x2), axis=-1, keepdims=True)
        r = jnp.where(even, lax.rsqrt(se * (1.0 / NA_HD) + EPS), lax.rsqrt(so * (1.0 / NA_HD) + EPS))
        return x * r * g

    gq, gk = gq_ref[...], gk_ref[...]
    for blk in range(d // LANES):
        sl = slice(blk * LANES, (blk + 1) * LANES)
        q_ref[:, sl] = pair_norm(qkv[:, sl], gq).astype(BF16)
        kn = pair_norm(qkv[:, d + blk * LANES:d + (blk + 1) * LANES], gk)
        k_ref[:, sl] = kn.astype(BF16)
        kf_ref[:, sl] = kn
    vv = qkv[:, 2 * d:]
    v_ref[...] = vv.astype(BF16)
    vf_ref[...] = vv


def _na_proj(tk, y, mod, gmix, win, gq2, gk2):
    d = D_MODEL
    ctx_spec = pl.BlockSpec((tk.tm, d), lambda t: (jnp.minimum(t, tk.nct), 0))
    return pl.pallas_call(
        _na_proj_kernel,
        out_shape=(jax.ShapeDtypeStruct((tk.n, d), BF16),) * 3
        + (jax.ShapeDtypeStruct((tk.nc + tk.tm, d), F32),) * 2,
        grid=(tk.nt,),
        in_specs=[_tok_spec(tk, d), _mod_spec(tk), _const_spec((1, d)), _const_spec(win.shape),
                  _const_spec((1, LANES)), _const_spec((1, LANES))],
        out_specs=(_tok_spec(tk, d),) * 3 + (ctx_spec,) * 2,
        compiler_params=_cparams(("arbitrary",)),
        name="na_proj",
    )(y, mod, gmix, win, gq2, gk2)


def _na_bias_tables(rel_bias, rows):
    ngroups = rows // NA_GROUP_ROWS
    wr = min(NA_WIN_R, rows)
    assert rows >= NA_SLAB_ROWS and wr == NA_WIN_R
    nr, nc = 2 * NA_WIN_R - 1, 2 * NA_WIN_C - 1

    def row_geometry(g):
        base = int(np.clip(NA_GROUP_ROWS * g - NA_WIN_R // 2, 0, rows - NA_SLAB_ROWS))
        r = NA_GROUP_ROWS * g + np.arange(NA_GROUP_ROWS)[:, None]
        kr = base + np.arange(NA_SLAB_ROWS)[None, :]
        rs = np.clip(r - wr // 2, 0, rows - wr)
        valid = (kr >= rs) & (kr < rs + wr)
        return valid, np.where(valid, kr - r + (NA_WIN_R - 1), -1)

    pats = [row_geometry(0), row_geometry(1), row_geometry(ngroups - 1)]
    for g in range(2, ngroups - 1):
        valid, dr = row_geometry(g)
        assert np.array_equal(valid, pats[1][0]) and np.array_equal(dr, pats[1][1])
    rvalid = np.stack([p[0] for p in pats])
    rsel = (np.stack([p[1] for p in pats])[..., None] == np.arange(nr)).astype(np.float32)
    c = np.arange(GRID_W)[:, None]
    kc = np.arange(GRID_W)[None, :]
    cs = np.clip(c - NA_WIN_C // 2, 0, GRID_W - NA_WIN_C)
    cvalid = (kc >= cs) & (kc < cs + NA_WIN_C)
    csel = (np.where(cvalid, kc - c + (NA_WIN_C - 1), -1)[..., None] == np.arange(nc)).astype(np.float32)
    tab = jnp.einsum("pair,hrd,cCd->hpaciC", rsel, rel_bias, csel, precision=lax.Precision.HIGHEST)
    valid = rvalid[:, :, None, :, None] & cvalid[None, None, :, None, :]
    tab = jnp.where(valid[None], tab, NEG_BIG)
    return tab.reshape(rel_bias.shape[0], 3, NA_GROUP_ROWS * GRID_W, NA_SLAB_ROWS * GRID_W)


def _na_lat_kernel(q_ref, k_ref, v_ref, kc_ref, vc_ref, b_ref, o_ref, *, scale, rows):
    g = pl.program_id(2)
    base = jnp.clip(NA_GROUP_ROWS * g - NA_WIN_R // 2, 0, rows - NA_SLAB_ROWS)
    start = pl.multiple_of(base * GRID_W, GRID_W)
    nkeys = NA_SLAB_ROWS * GRID_W
    kw = k_ref[pl.ds(start, nkeys), :]
    vw = v_ref[pl.ds(start, nkeys), :]
    kc = kc_ref[...]
    vc = vc_ref[...]
    q = q_ref[...]
    lane = lax.broadcasted_iota(jnp.int32, (1, LANES), 1)
    outs = []
    for e in range(2):
        mine = (lane < NA_HD) if e == 0 else (lane >= NA_HD)
        qe = jnp.where(mine, q, jnp.zeros_like(q))
        sc = lax.dot_general(qe, kc, _NT, preferred_element_type=F32) * scale
        sw = lax.dot_general(qe, kw, _NT, preferred_element_type=F32) * scale + b_ref[e]
        m = jnp.maximum(jnp.max(sc, axis=-1, keepdims=True), jnp.max(sw, axis=-1, keepdims=True))
        pc = jnp.exp(sc - m)
        pw = jnp.exp(sw - m)
        l = jnp.sum(pc, axis=-1, keepdims=True) + jnp.sum(pw, axis=-1, keepdims=True)
        o = (jnp.dot(pc.astype(BF16), vc, preferred_element_type=F32)
             + jnp.dot(pw.astype(BF16), vw, preferred_element_type=F32))
        outs.append(o / l)
    o_ref[...] = jnp.where(lane < NA_HD, outs[0], outs[1]).astype(o_ref.dtype)


def _na_latent(tk, q, k, v, kc, vc, bias):
    rows = tk.dec_seq // GRID_W
    ngroups = rows // NA_GROUP_ROWS
    gq = NA_GROUP_ROWS * GRID_W
    nkeys = NA_SLAB_ROWS * GRID_W
    hp = NA_HEADS // 2
    past = kc.shape[2]
    qo = tk.nc // gq
    ko = tk.nc // tk.dec_seq
    assert tk.nc % tk.dec_seq == 0

    def pat(g):
        return jnp.where(g == 0, 0, jnp.where(g == ngroups - 1, 2, 1))

    return pl.pallas_call(
        functools.partial(_na_lat_kernel, scale=NA_HD ** -0.5, rows=rows),
        out_shape=jax.ShapeDtypeStruct((tk.nl, D_MODEL), BF16),
        grid=(tk.dec_batch, hp, ngroups),
        in_specs=[pl.BlockSpec((gq, LANES), lambda b, h, g: (qo + b * ngroups + g, h)),
                  pl.BlockSpec((tk.dec_seq, LANES), lambda b, h, g: (ko + b, h)),
                  pl.BlockSpec((tk.dec_seq, LANES), lambda b, h, g: (ko + b, h)),
                  pl.BlockSpec((None, None, past, LANES), lambda b, h, g: (b, h, 0, 0)),
                  pl.BlockSpec((None, None, past, LANES), lambda b, h, g: (b, h, 0, 0)),
                  pl.BlockSpec((2, None, gq, nkeys), lambda b, h, g: (h, pat(g), 0, 0))],
        out_specs=pl.BlockSpec((gq, LANES), lambda b, h, g: (b * ngroups + g, h)),
        compiler_params=_cparams(("parallel", "parallel", "arbitrary")),
        name="na_latent",
    )(q, k, v, kc, vc, bias)


def _out_proj_kernel(y_ref, o_ref, w_ref, mod_ref, out_ref):
    out_ref[...] = y_ref[...] + mod_ref[2:3, :] * jnp.dot(o_ref[...], w_ref[...], preferred_element_type=F32)


def _out_proj(tk, y, o, w_o, mod):
    d = D_MODEL
    return pl.pallas_call(
        _out_proj_kernel,
        out_shape=jax.ShapeDtypeStruct((tk.n, d), F32),
        grid=(tk.nt,),
        in_specs=[_tok_spec(tk, d), _tok_spec(tk, d), _const_spec((d, d)), _mod_spec(tk)],
        out_specs=_tok_spec(tk, d),
        compiler_params=_cparams(("parallel",)),
        name="out_proj",
    )(y, o, w_o, mod)


def _peer_pairs():
    return [(k, l) for k in range(1, PEER_NTOP + 1) for l in range(1, PEER_NTOP // k + 1)]


def _extract_sorted(x, n):
    out = []
    for _ in range(n):
        m = jnp.max(x, axis=0, keepdims=True)
        out.append(m)
        x = jnp.where(x >= m, -jnp.inf, x)
    return out


def _peer_router_kernel(y_ref, mod_ref, gffn_ref, wqh_ref, wql_ref, skh_ref, skl_ref,
                        h_ref, th_ref, c_ref, s1_ref, e1_ref, s_scr, top_scr, cand_scr):
    h = _modulate(y_ref[...], gffn_ref[...], mod_ref[3:4, :], mod_ref[4:5, :])
    hh, hl = _split_bf16(h)
    h_ref[...] = hh
    wqh = wqh_ref[...]
    q = (jnp.dot(hh, wqh, preferred_element_type=F32) + jnp.dot(hh, wql_ref[...], preferred_element_type=F32)
         + jnp.dot(hl, wqh, preferred_element_type=F32))
    nlist = 2 * PEER_HEADS
    for n in range(nlist):
        qh, ql = _split_bf16(q[:, n * LANES:(n + 1) * LANES])
        kh = skh_ref[n]
        s_scr[n] = (lax.dot_general(kh, qh, _NT, preferred_element_type=F32)
                    + lax.dot_general(skl_ref[n], qh, _NT, preferred_element_type=F32)
                    + lax.dot_general(kh, ql, _NT, preferred_element_type=F32))

    def extract(n, carry):
        for r, m in enumerate(_extract_sorted(s_scr[n], PEER_NTOP)):
            top_scr[n, r:r + 1, :] = m
        return carry

    lax.fori_loop(0, nlist, extract, 0)

    pairs = _peer_pairs()
    npad = cand_scr.shape[0]

    def head(hd, carry):
        a = top_scr[2 * hd]
        b = top_scr[2 * hd + 1]
        for r, (k, l) in enumerate(pairs):
            cand_scr[r:r + 1, :] = a[k - 1:k, :] + b[l - 1:l, :]
        cand_scr[len(pairs):npad, :] = jnp.full((npad - len(pairs), cand_scr.shape[1]), -jnp.inf, F32)
        ts = _extract_sorted(cand_scr[...], PEER_NTOP)
        z = jnp.zeros_like(ts[0])
        for t in ts[:PEER_TOPK]:
            z = z + jnp.exp(t - ts[0])
        tau = 0.5 * (ts[PEER_TOPK - 1] + ts[PEER_TOPK])
        s0 = s_scr[2 * hd]
        s1 = s_scr[2 * hd + 1]
        th_ref[hd] = tau - s0
        c_ref[hd] = jnp.exp(s0 - a[0:1, :]) * (1.0 / z)
        s1_ref[hd] = s1
        e1_ref[hd] = jnp.exp(s1 - b[0:1, :])
        return carry

    lax.fori_loop(0, PEER_HEADS, head, 0)


def _peer_router(tk, y, mod, gffn, wqh, wql, skh, skl):
    d = D_MODEL
    nlist = 2 * PEER_HEADS
    npairs = len(_peer_pairs())
    rspec = pl.BlockSpec((PEER_HEADS, PEER_KEYS, tk.tm), lambda t: (0, 0, t))
    rshape = jax.ShapeDtypeStruct((PEER_HEADS, PEER_KEYS, tk.n), F32)
    return pl.pallas_call(
        _peer_router_kernel,
        out_shape=(jax.ShapeDtypeStruct((tk.n, d), BF16), rshape, rshape, rshape, rshape),
        grid=(tk.nt,),
        in_specs=[_tok_spec(tk, d), _mod_spec(tk), _const_spec((1, d)),
                  _const_spec(wqh.shape), _const_spec(wql.shape),
                  _const_spec(skh.shape), _const_spec(skl.shape)],
        out_specs=(_tok_spec(tk, d), rspec, rspec, rspec, rspec),
        scratch_shapes=[pltpu.VMEM((nlist, PEER_KEYS, tk.tm), F32),
                        pltpu.VMEM((nlist, 24, tk.tm), F32),
                        pltpu.VMEM((-(-npairs // 8) * 8, tk.tm), F32)],
        compiler_params=_cparams(("parallel",)),
        name="peer_router",
    )(y, mod, gffn, wqh, wql, skh, skl)


def _peer_dense_kernel(y_ref, mod_ref, h_ref, th_ref, c_ref, s1_ref, e1_ref, u_ref, vt_ref,
                       out_ref, s_scr, p_scr, acc_scr):
    eb = pl.program_id(1)

    @pl.when(eb == 0)
    def _():
        acc_scr[...] = jnp.zeros(acc_scr.shape, F32)

    s_scr[...] = lax.dot_general(u_ref[...], h_ref[...], _NT, preferred_element_type=F32)
    rows_per_block = s_scr.shape[0] // PEER_KEYS

    def row(il, carry, ls):
        i = eb * rows_per_block + il
        r0 = pl.multiple_of(il * PEER_KEYS, PEER_KEYS)
        s = s_scr[pl.ds(r0, PEER_KEYS), ls]
        act = s * (lax.erf(s * (2.0 ** -0.5)) + 1.0) * 0.5
        w = jnp.zeros_like(s)
        for hd in range(PEER_HEADS):
            th = th_ref[hd, pl.ds(i, 1), ls]
            cc = c_ref[hd, pl.ds(i, 1), ls]
            w = w + jnp.where(s1_ref[hd, :, ls] >= th, e1_ref[hd, :, ls], 0.0) * cc
        p_scr[pl.ds(r0, PEER_KEYS), ls] = (w * act).astype(BF16)
        return carry

    for lc in range(s_scr.shape[1] // PEER_LANE_CHUNK):
        ls = slice(lc * PEER_LANE_CHUNK, (lc + 1) * PEER_LANE_CHUNK)
        lax.fori_loop(0, rows_per_block, functools.partial(row, ls=ls), 0)
    acc_scr[...] += jnp.dot(vt_ref[...], p_scr[...], preferred_element_type=F32)

    @pl.when(eb == pl.num_programs(1) - 1)
    def _():
        out_ref[...] = y_ref[...] + mod_ref[5:6, :] * acc_scr[...].T


def _peer_dense(tk, y, mod, h, th, c, s1, e1, u_bf, vt_bf):
    d = D_MODEL
    nexp = u_bf.shape[0]
    eb = PEER_EXPERT_BLOCK
    tm = PEER_TOKEN_TILE
    assert tk.nc % tm == 0 and tk.dec_seq % tm == 0
    tok = lambda width: pl.BlockSpec((tm, width), lambda t, e: (t, 0))
    rspec = pl.BlockSpec((PEER_HEADS, PEER_KEYS, tm), lambda t, e: (0, 0, t))
    return pl.pallas_call(
        _peer_dense_kernel,
        out_shape=jax.ShapeDtypeStruct((tk.n, d), F32),
        grid=(tk.n // tm, nexp // eb),
        in_specs=[tok(d), pl.BlockSpec((None, 6, d), lambda t, e: (tk.cond_row(t, tm), 0, 0)), tok(d),
                  rspec, rspec, rspec, rspec,
                  pl.BlockSpec((eb, d), lambda t, e: (e, 0)),
                  pl.BlockSpec((d, eb), lambda t, e: (0, e))],
        out_specs=tok(d),
        scratch_shapes=[pltpu.VMEM((eb, tm), F32), pltpu.VMEM((eb, tm), BF16),
                        pltpu.VMEM((d, tm), F32)],
        compiler_params=_cparams(("parallel", "arbitrary")),
        name="peer_dense",
    )(y, mod, h, th, c, s1, e1, u_bf, vt_bf)


def kernel(x_prompt, x_sample, cache_mla_ckv, cache_mla_krope, cache_na_k, cache_na_v, c, c_ctx, w_mod, b_mod, g_norm_mix, g_norm_ffn, mla_w_in, mla_g_q_a, mla_g_kv_a, mla_w_q_b, mla_w_kv_b, mla_g_q, mla_g_k, mla_w_o, na_w_in, na_g_q, na_g_k, na_rel_bias, na_w_o, peer_w_q, peer_sub_keys, peer_u, peer_v):
    batch, seq, d = x_prompt.shape
    dec_batch, dec_seq, _ = x_sample.shape
    past = cache_mla_ckv.shape[2]
    depth = w_mod.shape[0]
    tk = _Tokens(batch, seq, dec_batch, dec_seq)
    assert dec_batch + 1 <= 8 and d == D_MODEL

    y = jnp.concatenate([x_prompt.reshape(tk.nc, d), x_sample.reshape(tk.nl, d)], axis=0)
    cond8 = jnp.zeros((8, d), F32).at[:dec_batch].set(c).at[dec_batch].set(c_ctx)
    mod_all = _adaln(cond8, w_mod, b_mod).reshape(depth, 8, 6, d)
    rope_tabs = _rope_tables(tk)
    rows = dec_seq // GRID_W

    new_ckv, new_krope, new_k, new_v = [], [], [], []
    for i in range(depth):
        j = i // 2
        mod = mod_all[i]
        gmix = g_norm_mix[i].reshape(1, d)
        if i % 2 == 0:
            w = _mla_weights(mla_w_in[j], mla_g_q_a[j], mla_g_kv_a[j], mla_w_q_b[j], mla_w_kv_b[j],
                             mla_g_q[j], mla_g_k[j])
            q, k, v, ckv_n, kr = _mla_proj(tk, y, mod, gmix, w, rope_tabs)
            new_ckv.append(ckv_n[:tk.nc].reshape(batch, seq, MLA_KV_RANK))
            new_krope.append(kr[:tk.nc, MLA_NOPE:MLA_QK].reshape(batch, seq, MLA_ROPE))
            kr_c = jnp.pad(cache_mla_krope[:, j], ((0, 0), (0, 0), (MLA_NOPE, LANES - MLA_QK)))
            kc, vc = _mla_cache_kv(cache_mla_ckv[:, j].reshape(dec_batch * past, MLA_KV_RANK),
                                   kr_c.reshape(dec_batch * past, LANES), w)
            scale = MLA_QK ** -0.5
            o_ctx = _flash(q, k, v, nb=batch, tq_len=seq, tk_len=seq, q_off=0, k_off=0,
                           width=2 * LANES, scale=scale, tq=seq, tkb=seq)
            kfull = jnp.concatenate([kc.reshape(dec_batch, past, -1),
                                     k[tk.nc:].reshape(dec_batch, dec_seq, -1)], axis=1)
            vfull = jnp.concatenate([vc.reshape(dec_batch, past, -1),
                                     v[tk.nc:].reshape(dec_batch, dec_seq, -1)], axis=1)
            klen = past + dec_seq
            o_lat = _flash(q, kfull.reshape(dec_batch * klen, -1), vfull.reshape(dec_batch * klen, -1),
                           nb=dec_batch, tq_len=dec_seq, tk_len=klen, q_off=tk.nc, k_off=0,
                           width=2 * LANES, scale=scale, tq=512, tkb=512)
            w_o = mla_w_o[j].astype(BF16)
        else:
            gq2 = jnp.tile(na_g_q[j], 2).reshape(1, LANES)
            gk2 = jnp.tile(na_g_k[j], 2).reshape(1, LANES)
            q, k, v, kf, vf = _na_proj(tk, y, mod, gmix, na_w_in[j].astype(BF16), gq2, gk2)
            new_k.append(kf[:tk.nc].reshape(batch, seq, NA_HEADS, NA_HD).transpose(0, 2, 1, 3))
            new_v.append(vf[:tk.nc].reshape(batch, seq, NA_HEADS, NA_HD).transpose(0, 2, 1, 3))
            scale = NA_HD ** -0.5
            o_ctx = _flash(q, k, v, nb=batch, tq_len=seq, tk_len=seq, q_off=0, k_off=0,
                           width=LANES, scale=scale, tq=seq, tkb=seq)

            def pair_layout(x):
                b_, h_, l_, e_ = x.shape
                return x.reshape(b_, h_ // 2, 2, l_, e_).transpose(0, 1, 3, 2, 4).reshape(b_, h_ // 2, l_, 2 * e_)

            kc = pair_layout(cache_na_k[:, j]).astype(BF16)
            vc = pair_layout(cache_na_v[:, j]).astype(BF16)
            bias = _na_bias_tables(na_rel_bias[j], rows)
            o_lat = _na_latent(tk, q, k, v, kc, vc, bias)
            w_o = na_w_o[j].astype(BF16)
        o = jnp.concatenate([o_ctx, o_lat], axis=0)
        y = _out_proj(tk, y, o, w_o, mod)

        wqh, wql = _split_bf16(peer_w_q[i])
        skh, skl = _split_bf16(peer_sub_keys[i].reshape(2 * PEER_HEADS, PEER_KEYS, PEER_KEYS))
        h, th, cc, s1, e1 = _peer_router(tk, y, mod, g_norm_ffn[i].reshape(1, d), wqh, wql, skh, skl)
        u_bf = peer_u[i].astype(BF16)
        vt_bf = peer_v[i].T.astype(BF16)
        y = _peer_dense(tk, y, mod, h, th, cc, s1, e1, u_bf, vt_bf)

    return (y[:tk.nc].reshape(batch, seq, d), y[tk.nc:].reshape(dec_batch, dec_seq, d),
            jnp.stack(new_ckv, axis=1), jnp.stack(new_krope, axis=1),
            jnp.stack(new_k, axis=1), jnp.stack(new_v, axis=1))
```

```python
import functools

import numpy as np
import jax
import jax.numpy as jnp
from jax import lax
from jax.experimental import pallas as pl
from jax.experimental.pallas import tpu as pltpu

F32 = jnp.float32
BF16 = jnp.bfloat16

D_MODEL = 1024
GRID_W = 64
MLA_HEADS = 16
MLA_NOPE = 64
MLA_ROPE = 32
MLA_QK = MLA_NOPE + MLA_ROPE
MLA_Q_RANK = 256
MLA_KV_RANK = 128
NA_HEADS = 16
NA_HD = 64
NA_WIN_R = 8
NA_WIN_C = 16
PEER_HEADS = 8
PEER_KEYS = 128
PEER_TOPK = 16
ROPE_BASE = 10000.0
EPS = 1e-6

LANES = 128
TOKEN_TILE = 256
NA_GROUP_ROWS = 4
NA_SLAB_ROWS = NA_WIN_R + NA_GROUP_ROWS
PEER_EXPERT_BLOCK = 1024
PEER_TOKEN_TILE = 512
PEER_LANE_CHUNK = 256
PEER_NTOP = PEER_TOPK + 1
NEG_BIG = -1e30
VMEM_LIMIT = 56 * 1024 * 1024

_NT = (((1,), (1,)), ((), ()))


def _cparams(sem):
    return pltpu.CompilerParams(dimension_semantics=sem, vmem_limit_bytes=VMEM_LIMIT)


def _modulate(y, g, shift, scale):
    ms = jnp.mean(y * y, axis=-1, keepdims=True)
    return (y * lax.rsqrt(ms + EPS)) * g * (1.0 + scale) + shift


def _rms(x, n):
    return x * lax.rsqrt(jnp.sum(x * x, axis=-1, keepdims=True) * (1.0 / n) + EPS)


def _split_bf16(x):
    hi = x.astype(BF16)
    lo = (x - hi.astype(F32)).astype(BF16)
    return hi, lo


def _adaln_kernel(c_ref, w_ref, b_ref, o_ref):
    c = c_ref[...]
    s = c / (1.0 + jnp.exp(-c))
    o_ref[...] = jnp.dot(s, w_ref[...], preferred_element_type=F32,
                         precision=lax.Precision.HIGHEST) + b_ref[...]


def _adaln(cond8, w_mod, b_mod):
    depth, d, d6 = w_mod.shape
    nj = d6 // d
    return pl.pallas_call(
        _adaln_kernel,
        out_shape=jax.ShapeDtypeStruct((depth, 8, d6), F32),
        grid=(depth, nj),
        in_specs=[pl.BlockSpec((8, d), lambda l, j: (0, 0)),
                  pl.BlockSpec((None, d, d), lambda l, j: (l, 0, j)),
                  pl.BlockSpec((None, 1, d), lambda l, j: (l, 0, j))],
        out_specs=pl.BlockSpec((None, 8, d), lambda l, j: (l, 0, j)),
        compiler_params=_cparams(("parallel", "parallel")),
        name="adaln",
    )(cond8, w_mod, b_mod.reshape(depth, 1, d6))


class _Tokens:
    def __init__(self, batch, seq, dec_batch, dec_seq):
        self.batch, self.seq, self.dec_batch, self.dec_seq = batch, seq, dec_batch, dec_seq
        self.nc = batch * seq
        self.nl = dec_batch * dec_seq
        self.n = self.nc + self.nl
        tm = TOKEN_TILE
        assert seq % tm == 0 and dec_seq % tm == 0
        self.tm = tm
        self.nct = self.nc // tm
        self.tpb = dec_seq // tm
        self.nt = self.n // tm
        self.ctx_row = dec_batch

    def cond_row(self, t, tm=None):
        tm = self.tm if tm is None else tm
        nct, tpb = self.nc // tm, self.dec_seq // tm
        lat = lax.div(jnp.maximum(t - nct, 0), tpb)
        return jnp.where(t < nct, self.ctx_row, lat)

    def rope_blk(self, t):
        lat = lax.rem(jnp.maximum(t - self.nct, 0), self.tpb)
        return jnp.where(t < self.nct, self.tpb, lat)


def _tok_spec(tk, width):
    return pl.BlockSpec((tk.tm, width), lambda t: (t, 0))


def _mod_spec(tk):
    return pl.BlockSpec((None, 6, D_MODEL), lambda t: (tk.cond_row(t), 0, 0))


def _const_spec(shape):
    nd = len(shape)
    return pl.BlockSpec(shape, lambda t: (0,) * nd)


def _rope(x, cos, sina, sinb):
    return x * cos + pltpu.roll(x, LANES - 8, 1) * sina + pltpu.roll(x, 8, 1) * sinb


def _mla_kv_tail(ckv_n, kr, wkn_ref, wv_ref, gk_ref, rope, k_ref, v_ref):
    cb = ckv_n.astype(BF16)
    kn = jnp.dot(cb, wkn_ref[...], preferred_element_type=F32)
    v_ref[...] = jnp.dot(cb, wv_ref[...], preferred_element_type=F32).astype(BF16)
    gk = gk_ref[...]
    for hd in range(MLA_HEADS):
        sl = slice(hd * LANES, (hd + 1) * LANES)
        kh = _rms(kn[:, sl] + kr, MLA_QK) * gk
        if rope is not None:
            kh = _rope(kh, *rope)
        k_ref[:, sl] = kh.astype(BF16)


def _mla_proj_kernel(y_ref, mod_ref, gmix_ref, win_ref, gqa_ref, gkva_ref, wqb_ref, gq_ref,
                     wkn_ref, wv_ref, gk_ref, cos_ref, sina_ref, sinb_ref,
                     q_ref, k_ref, v_ref, ckv_ref, kr_ref):
    h = _modulate(y_ref[...], gmix_ref[...], mod_ref[0:1, :], mod_ref[1:2, :])
    a = jnp.dot(h.astype(BF16), win_ref[...], preferred_element_type=F32)
    qa = a[:, :MLA_Q_RANK]
    ckv = a[:, MLA_Q_RANK:MLA_Q_RANK + MLA_KV_RANK]
    kr = a[:, MLA_Q_RANK + MLA_KV_RANK:]
    qn = _rms(qa, MLA_Q_RANK) * gqa_ref[...]
    ckv_n = _rms(ckv, MLA_KV_RANK) * gkva_ref[...]
    ckv_ref[...] = ckv_n
    kr_ref[...] = kr
    rope = (cos_ref[...], sina_ref[...], sinb_ref[...])
    q = jnp.dot(qn.astype(BF16), wqb_ref[...], preferred_element_type=F32)
    gq = gq_ref[...]
    for hd in range(MLA_HEADS):
        sl = slice(hd * LANES, (hd + 1) * LANES)
        qh = _rope(_rms(q[:, sl], MLA_QK) * gq, *rope)
        q_ref[:, sl] = qh.astype(BF16)
    _mla_kv_tail(ckv_n, kr, wkn_ref, wv_ref, gk_ref, rope, k_ref, v_ref)


def _mla_cache_kernel(ckv_ref, kr_ref, wkn_ref, wv_ref, gk_ref, k_ref, v_ref):
    _mla_kv_tail(ckv_ref[...], kr_ref[...], wkn_ref, wv_ref, gk_ref, None, k_ref, v_ref)


def _mla_weights(w_in, g_q_a, g_kv_a, w_q_b, w_kv_b, g_q, g_k):
    d = w_in.shape[0]
    w_kr = w_in[:, MLA_Q_RANK + MLA_KV_RANK:]
    w_kr = jnp.pad(w_kr, ((0, 0), (MLA_NOPE, LANES - MLA_QK)))
    win = jnp.concatenate([w_in[:, :MLA_Q_RANK + MLA_KV_RANK], w_kr], axis=1).astype(BF16)
    wqb = w_q_b.reshape(MLA_Q_RANK, MLA_HEADS, MLA_QK)
    wqb = jnp.pad(wqb, ((0, 0), (0, 0), (0, LANES - MLA_QK))).reshape(MLA_Q_RANK, MLA_HEADS * LANES)
    wkv = w_kv_b.reshape(MLA_KV_RANK, MLA_HEADS, 2 * MLA_NOPE)
    wkn = jnp.pad(wkv[:, :, :MLA_NOPE], ((0, 0), (0, 0), (0, LANES - MLA_NOPE)))
    wkn = wkn.reshape(MLA_KV_RANK, MLA_HEADS * LANES)
    wv = wkv[:, :, MLA_NOPE:].reshape(MLA_KV_RANK, MLA_HEADS * MLA_NOPE)
    pad = (0, LANES - MLA_QK)
    return dict(win=win, gqa=g_q_a.reshape(1, -1), gkva=g_kv_a.reshape(1, -1),
                wqb=wqb.astype(BF16), gq=jnp.pad(g_q, pad).reshape(1, LANES),
                wkn=wkn.astype(BF16), wv=wv.astype(BF16), gk=jnp.pad(g_k, pad).reshape(1, LANES))


def _rope_tables(tk):
    t = jnp.arange(tk.dec_seq)
    row = (t // GRID_W).astype(F32)
    col = (t % GRID_W).astype(F32)
    half = MLA_ROPE // 2
    inv = 1.0 / (ROPE_BASE ** (jnp.arange(0, half, 2, dtype=F32) / half))
    ar = row[:, None] * inv[None, :]
    ac = col[:, None] * inv[None, :]
    ang = jnp.concatenate([ar, ar, ac, ac], axis=-1)
    cos, sin = jnp.cos(ang), jnp.sin(ang)
    grp = (np.arange(MLA_ROPE) // 8) % 2
    sina = jnp.where(grp == 0, -sin, 0.0)
    sinb = jnp.where(grp == 1, sin, 0.0)

    def place(x, fill):
        full = jnp.full((tk.dec_seq + tk.tm, LANES), fill, F32)
        full = full.at[:tk.dec_seq, MLA_NOPE:MLA_QK].set(x)
        return full.at[tk.dec_seq:, MLA_NOPE:MLA_QK].set(fill)

    return place(cos, 1.0), place(sina, 0.0), place(sinb, 0.0)


def _mla_proj(tk, y, mod, gmix, w, rope_tabs):
    hw = MLA_HEADS * LANES
    rope_spec = pl.BlockSpec((tk.tm, LANES), lambda t: (tk.rope_blk(t), 0))
    names = ("win", "gqa", "gkva", "wqb", "gq", "wkn", "wv", "gk")
    return pl.pallas_call(
        _mla_proj_kernel,
        out_shape=(jax.ShapeDtypeStruct((tk.n, hw), BF16),
                   jax.ShapeDtypeStruct((tk.n, hw), BF16),
                   jax.ShapeDtypeStruct((tk.n, MLA_HEADS * MLA_NOPE), BF16),
                   jax.ShapeDtypeStruct((tk.n, MLA_KV_RANK), F32),
                   jax.ShapeDtypeStruct((tk.n, LANES), F32)),
        grid=(tk.nt,),
        in_specs=[_tok_spec(tk, D_MODEL), _mod_spec(tk), _const_spec((1, D_MODEL))]
        + [_const_spec(w[n].shape) for n in names] + [rope_spec] * 3,
        out_specs=(_tok_spec(tk, hw), _tok_spec(tk, hw), _tok_spec(tk, MLA_HEADS * MLA_NOPE),
                   _tok_spec(tk, MLA_KV_RANK), _tok_spec(tk, LANES)),
        compiler_params=_cparams(("parallel",)),
        name="mla_proj",
    )(y, mod, gmix, *[w[n] for n in names], *rope_tabs)


def _mla_cache_kv(ckv, kr128, w):
    n = ckv.shape[0]
    tm = TOKEN_TILE
    hw = MLA_HEADS * LANES
    spec = lambda width: pl.BlockSpec((tm, width), lambda t: (t, 0))
    return pl.pallas_call(
        _mla_cache_kernel,
        out_shape=(jax.ShapeDtypeStruct((n, hw), BF16),
                   jax.ShapeDtypeStruct((n, MLA_HEADS * MLA_NOPE), BF16)),
        grid=(n // tm,),
        in_specs=[spec(MLA_KV_RANK), spec(LANES)] + [_const_spec(w[k].shape) for k in ("wkn", "wv", "gk")],
        out_specs=(spec(hw), spec(MLA_HEADS * MLA_NOPE)),
        compiler_params=_cparams(("parallel",)),
        name="mla_cache_kv",
    )(ckv, kr128, w["wkn"], w["wv"], w["gk"])


def _flash_kernel(q_ref, k_ref, v_ref, o_ref, m_scr, l_scr, acc_scr, *, scale, width):
    ki = pl.program_id(3)

    @pl.when(ki == 0)
    def _():
        m_scr[...] = jnp.full(m_scr.shape, -jnp.inf, F32)
        l_scr[...] = jnp.zeros(l_scr.shape, F32)
        acc_scr[...] = jnp.zeros(acc_scr.shape, F32)

    q = q_ref[...]
    k = k_ref[...]
    v = v_ref[...]
    lane = lax.broadcasted_iota(jnp.int32, (1, width), 1)
    for e in range(2):
        qe = jnp.where((lane >= e * (width // 2)) & (lane < (e + 1) * (width // 2)), q, jnp.zeros_like(q))
        s = lax.dot_general(qe, k, _NT, preferred_element_type=F32) * scale
        m_prev = m_scr[e]
        m_new = jnp.maximum(m_prev, jnp.max(s, axis=-1, keepdims=True))
        alpha = jnp.exp(m_prev - m_new)
        p = jnp.exp(s - m_new)
        l_scr[e] = alpha * l_scr[e] + jnp.sum(p, axis=-1, keepdims=True)
        acc_scr[e] = alpha * acc_scr[e] + jnp.dot(p.astype(BF16), v, preferred_element_type=F32)
        m_scr[e] = m_new

    @pl.when(ki == pl.num_programs(3) - 1)
    def _():
        lane_o = lax.broadcasted_iota(jnp.int32, (1, LANES), 1)
        o0 = acc_scr[0] / l_scr[0]
        o1 = acc_scr[1] / l_scr[1]
        o_ref[...] = jnp.where(lane_o < LANES // 2, o0, o1).astype(o_ref.dtype)


def _flash(q, k, v, *, nb, tq_len, tk_len, q_off, k_off, width, scale, tq, tkb):
    hp = q.shape[1] // width
    nq, nk = tq_len // tq, tk_len // tkb
    qo, ko = q_off // tq, k_off // tkb
    assert q_off % tq == 0 and k_off % tkb == 0
    return pl.pallas_call(
        functools.partial(_flash_kernel, scale=scale, width=width),
        out_shape=jax.ShapeDtypeStruct((nb * tq_len, hp * LANES), BF16),
        grid=(nb, hp, nq, nk),
        in_specs=[pl.BlockSpec((tq, width), lambda b, h, i, j: (qo + b * nq + i, h)),
                  pl.BlockSpec((tkb, width), lambda b, h, i, j: (ko + b * nk + j, h)),
                  pl.BlockSpec((tkb, LANES), lambda b, h, i, j: (ko + b * nk + j, h))],
        out_specs=pl.BlockSpec((tq, LANES), lambda b, h, i, j: (b * nq + i, h)),
        scratch_shapes=[pltpu.VMEM((2, tq, 1), F32), pltpu.VMEM((2, tq, 1), F32),
                        pltpu.VMEM((2, tq, LANES), F32)],
        compiler_params=_cparams(("parallel", "parallel", "parallel", "arbitrary")),
        name="flash_attn",
    )(q, k, v)


def _na_proj_kernel(y_ref, mod_ref, gmix_ref, win_ref, gq_ref, gk_ref,
                    q_ref, k_ref, v_ref, kf_ref, vf_ref):
    h = _modulate(y_ref[...], gmix_ref[...], mod_ref[0:1, :], mod_ref[1:2, :])
    qkv = jnp.dot(h.astype(BF16), win_ref[...], preferred_element_type=F32)
    d = D_MODEL
    even = lax.broadcasted_iota(jnp.int32, (1, LANES), 1) < NA_HD

    def pair_norm(x, g):
        x2 = x * x
        se = jnp.sum(jnp.where(even, x2, 0.0), axis=-1, keepdims=True)
        so = jnp.sum(jnp.where(even, 0.0, x2), axis=-1, keepdims=True)
        r = jnp.where(even, lax.rsqrt(se * (1.0 / NA_HD) + EPS), lax.rsqrt(so * (1.0 / NA_HD) + EPS))
        return x * r * g

    gq, gk = gq_ref[...], gk_ref[...]
    for blk in range(d // LANES):
        sl = slice(blk * LANES, (blk + 1) * LANES)
        q_ref[:, sl] = pair_norm(qkv[:, sl], gq).astype(BF16)
        kn = pair_norm(qkv[:, d + blk * LANES:d + (blk + 1) * LANES], gk)
        k_ref[:, sl] = kn.astype(BF16)
        kf_ref[:, sl] = kn
    vv = qkv[:, 2 * d:]
    v_ref[...] = vv.astype(BF16)
    vf_ref[...] = vv


def _na_proj(tk, y, mod, gmix, win, gq2, gk2):
    d = D_MODEL
    ctx_spec = pl.BlockSpec((tk.tm, d), lambda t: (jnp.minimum(t, tk.nct), 0))
    return pl.pallas_call(
        _na_proj_kernel,
        out_shape=(jax.ShapeDtypeStruct((tk.n, d), BF16),) * 3
        + (jax.ShapeDtypeStruct((tk.nc + tk.tm, d), F32),) * 2,
        grid=(tk.nt,),
        in_specs=[_tok_spec(tk, d), _mod_spec(tk), _const_spec((1, d)), _const_spec(win.shape),
                  _const_spec((1, LANES)), _const_spec((1, LANES))],
        out_specs=(_tok_spec(tk, d),) * 3 + (ctx_spec,) * 2,
        compiler_params=_cparams(("arbitrary",)),
        name="na_proj",
    )(y, mod, gmix, win, gq2, gk2)


def _na_bias_tables(rel_bias, rows):
    ngroups = rows // NA_GROUP_ROWS
    wr = min(NA_WIN_R, rows)
    assert rows >= NA_SLAB_ROWS and wr == NA_WIN_R
    nr, nc = 2 * NA_WIN_R - 1, 2 * NA_WIN_C - 1

    def row_geometry(g):
        base = int(np.clip(NA_GROUP_ROWS * g - NA_WIN_R // 2, 0, rows - NA_SLAB_ROWS))
        r = NA_GROUP_ROWS * g + np.arange(NA_GROUP_ROWS)[:, None]
        kr = base + np.arange(NA_SLAB_ROWS)[None, :]
        rs = np.clip(r - wr // 2, 0, rows - wr)
        valid = (kr >= rs) & (kr < rs + wr)
        return valid, np.where(valid, kr - r + (NA_WIN_R - 1), -1)

    pats = [row_geometry(0), row_geometry(1), row_geometry(ngroups - 1)]
    for g in range(2, ngroups - 1):
        valid, dr = row_geometry(g)
        assert np.array_equal(valid, pats[1][0]) and np.array_equal(dr, pats[1][1])
    rvalid = np.stack([p[0] for p in pats])
    rsel = (np.stack([p[1] for p in pats])[..., None] == np.arange(nr)).astype(np.float32)
    c = np.arange(GRID_W)[:, None]
    kc = np.arange(GRID_W)[None, :]
    cs = np.clip(c - NA_WIN_C // 2, 0, GRID_W - NA_WIN_C)
    cvalid = (kc >= cs) & (kc < cs + NA_WIN_C)
    csel = (np.where(cvalid, kc - c + (NA_WIN_C - 1), -1)[..., None] == np.arange(nc)).astype(np.float32)
    tab = jnp.einsum("pair,hrd,cCd->hpaciC", rsel, rel_bias, csel, precision=lax.Precision.HIGHEST)
    valid = rvalid[:, :, None, :, None] & cvalid[None, None, :, None, :]
    tab = jnp.where(valid[None], tab, NEG_BIG)
    return tab.reshape(rel_bias.shape[0], 3, NA_GROUP_ROWS * GRID_W, NA_SLAB_ROWS * GRID_W)


def _na_lat_kernel(q_ref, k_ref, v_ref, kc_ref, vc_ref, b_ref, o_ref, *, scale, rows):
    g = pl.program_id(2)
    base = jnp.clip(NA_GROUP_ROWS * g - NA_WIN_R // 2, 0, rows - NA_SLAB_ROWS)
    start = pl.multiple_of(base * GRID_W, GRID_W)
    nkeys = NA_SLAB_ROWS * GRID_W
    kw = k_ref[pl.ds(start, nkeys), :]
    vw = v_ref[pl.ds(start, nkeys), :]
    kc = kc_ref[...]
    vc = vc_ref[...]
    q = q_ref[...]
    lane = lax.broadcasted_iota(jnp.int32, (1, LANES), 1)
    outs = []
    for e in range(2):
        mine = (lane < NA_HD) if e == 0 else (lane >= NA_HD)
        qe = jnp.where(mine, q, jnp.zeros_like(q))
        sc = lax.dot_general(qe, kc, _NT, preferred_element_type=F32) * scale
        sw = lax.dot_general(qe, kw, _NT, preferred_element_type=F32) * scale + b_ref[e]
        m = jnp.maximum(jnp.max(sc, axis=-1, keepdims=True), jnp.max(sw, axis=-1, keepdims=True))
        pc = jnp.exp(sc - m)
        pw = jnp.exp(sw - m)
        l = jnp.sum(pc, axis=-1, keepdims=True) + jnp.sum(pw, axis=-1, keepdims=True)
        o = (jnp.dot(pc.astype(BF16), vc, preferred_element_type=F32)
             + jnp.dot(pw.astype(BF16), vw, preferred_element_type=F32))
        outs.append(o / l)
    o_ref[...] = jnp.where(lane < NA_HD, outs[0], outs[1]).astype(o_ref.dtype)


def _na_latent(tk, q, k, v, kc, vc, bias):
    rows = tk.dec_seq // GRID_W
    ngroups = rows // NA_GROUP_ROWS
    gq = NA_GROUP_ROWS * GRID_W
    nkeys = NA_SLAB_ROWS * GRID_W
    hp = NA_HEADS // 2
    past = kc.shape[2]
    qo = tk.nc // gq
    ko = tk.nc // tk.dec_seq
    assert tk.nc % tk.dec_seq == 0

    def pat(g):
        return jnp.where(g == 0, 0, jnp.where(g == ngroups - 1, 2, 1))

    return pl.pallas_call(
        functools.partial(_na_lat_kernel, scale=NA_HD ** -0.5, rows=rows),
        out_shape=jax.ShapeDtypeStruct((tk.nl, D_MODEL), BF16),
        grid=(tk.dec_batch, hp, ngroups),
        in_specs=[pl.BlockSpec((gq, LANES), lambda b, h, g: (qo + b * ngroups + g, h)),
                  pl.BlockSpec((tk.dec_seq, LANES), lambda b, h, g: (ko + b, h)),
                  pl.BlockSpec((tk.dec_seq, LANES), lambda b, h, g: (ko + b, h)),
                  pl.BlockSpec((None, None, past, LANES), lambda b, h, g: (b, h, 0, 0)),
                  pl.BlockSpec((None, None, past, LANES), lambda b, h, g: (b, h, 0, 0)),
                  pl.BlockSpec((2, None, gq, nkeys), lambda b, h, g: (h, pat(g), 0, 0))],
        out_specs=pl.BlockSpec((gq, LANES), lambda b, h, g: (b * ngroups + g, h)),
        compiler_params=_cparams(("parallel", "parallel", "arbitrary")),
        name="na_latent",
    )(q, k, v, kc, vc, bias)


def _out_proj_kernel(y_ref, o_ref, w_ref, mod_ref, out_ref):
    out_ref[...] = y_ref[...] + mod_ref[2:3, :] * jnp.dot(o_ref[...], w_ref[...], preferred_element_type=F32)


def _out_proj(tk, y, o, w_o, mod):
    d = D_MODEL
    return pl.pallas_call(
        _out_proj_kernel,
        out_shape=jax.ShapeDtypeStruct((tk.n, d), F32),
        grid=(tk.nt,),
        in_specs=[_tok_spec(tk, d), _tok_spec(tk, d), _const_spec((d, d)), _mod_spec(tk)],
        out_specs=_tok_spec(tk, d),
        compiler_params=_cparams(("parallel",)),
        name="out_proj",
    )(y, o, w_o, mod)


def _peer_pairs():
    return [(k, l) for k in range(1, PEER_NTOP + 1) for l in range(1, PEER_NTOP // k + 1)]


def _extract_sorted(x, n):
    out = []
    for _ in range(n):
        m = jnp.max(x, axis=0, keepdims=True)
        out.append(m)
        x = jnp.where(x >= m, -jnp.inf, x)
    return out


def _peer_router_kernel(y_ref, mod_ref, gffn_ref, wqh_ref, wql_ref, skh_ref, skl_ref,
                        h_ref, th_ref, c_ref, s1_ref, e1_ref, s_scr, top_scr, cand_scr):
    h = _modulate(y_ref[...], gffn_ref[...], mod_ref[3:4, :], mod_ref[4:5, :])
    hh, hl = _split_bf16(h)
    h_ref[...] = hh
    wqh = wqh_ref[...]
    q = (jnp.dot(hh, wqh, preferred_element_type=F32) + jnp.dot(hh, wql_ref[...], preferred_element_type=F32)
         + jnp.dot(hl, wqh, preferred_element_type=F32))
    nlist = 2 * PEER_HEADS
    for n in range(nlist):
        qh, ql = _split_bf16(q[:, n * LANES:(n + 1) * LANES])
        kh = skh_ref[n]
        s_scr[n] = (lax.dot_general(kh, qh, _NT, preferred_element_type=F32)
                    + lax.dot_general(skl_ref[n], qh, _NT, preferred_element_type=F32)
                    + lax.dot_general(kh, ql, _NT, preferred_element_type=F32))

    def extract(n, carry):
        for r, m in enumerate(_extract_sorted(s_scr[n], PEER_NTOP)):
            top_scr[n, r:r + 1, :] = m
        return carry

    lax.fori_loop(0, nlist, extract, 0)

    pairs = _peer_pairs()
    npad = cand_scr.shape[0]

    def head(hd, carry):
        a = top_scr[2 * hd]
        b = top_scr[2 * hd + 1]
        for r, (k, l) in enumerate(pairs):
            cand_scr[r:r + 1, :] = a[k - 1:k, :] + b[l - 1:l, :]
        cand_scr[len(pairs):npad, :] = jnp.full((npad - len(pairs), cand_scr.shape[1]), -jnp.inf, F32)
        ts = _extract_sorted(cand_scr[...], PEER_NTOP)
        z = jnp.zeros_like(ts[0])
        for t in ts[:PEER_TOPK]:
            z = z + jnp.exp(t - ts[0])
        tau = 0.5 * (ts[PEER_TOPK - 1] + ts[PEER_TOPK])
        s0 = s_scr[2 * hd]
        s1 = s_scr[2 * hd + 1]
        th_ref[hd] = tau - s0
        c_ref[hd] = jnp.exp(s0 - a[0:1, :]) * (1.0 / z)
        s1_ref[hd] = s1
        e1_ref[hd] = jnp.exp(s1 - b[0:1, :])
        return carry

    lax.fori_loop(0, PEER_HEADS, head, 0)


def _peer_router(tk, y, mod, gffn, wqh, wql, skh, skl):
    d = D_MODEL
    nlist = 2 * PEER_HEADS
    npairs = len(_peer_pairs())
    rspec = pl.BlockSpec((PEER_HEADS, PEER_KEYS, tk.tm), lambda t: (0, 0, t))
    rshape = jax.ShapeDtypeStruct((PEER_HEADS, PEER_KEYS, tk.n), F32)
    return pl.pallas_call(
        _peer_router_kernel,
        out_shape=(jax.ShapeDtypeStruct((tk.n, d), BF16), rshape, rshape, rshape, rshape),
        grid=(tk.nt,),
        in_specs=[_tok_spec(tk, d), _mod_spec(tk), _const_spec((1, d)),
                  _const_spec(wqh.shape), _const_spec(wql.shape),
                  _const_spec(skh.shape), _const_spec(skl.shape)],
        out_specs=(_tok_spec(tk, d), rspec, rspec, rspec, rspec),
        scratch_shapes=[pltpu.VMEM((nlist, PEER_KEYS, tk.tm), F32),
                        pltpu.VMEM((nlist, 24, tk.tm), F32),
                        pltpu.VMEM((-(-npairs // 8) * 8, tk.tm), F32)],
        compiler_params=_cparams(("parallel",)),
        name="peer_router",
    )(y, mod, gffn, wqh, wql, skh, skl)


def _peer_dense_kernel(y_ref, mod_ref, h_ref, *rest):
    nchunk = PEER_TOKEN_TILE // PEER_LANE_CHUNK
    router = [rest[a * nchunk:(a + 1) * nchunk] for a in range(4)]
    u_ref, vt_ref, out_ref, s_scr, p_scr, acc_scr = rest[4 * nchunk:]
    eb = pl.program_id(1)

    @pl.when(eb == 0)
    def _():
        acc_scr[...] = jnp.zeros(acc_scr.shape, F32)

    s_scr[...] = lax.dot_general(u_ref[...], h_ref[...], _NT, preferred_element_type=F32)
    rows_per_block = s_scr.shape[0] // PEER_KEYS

    def row(il, carry, lc):
        th_ref, c_ref, s1_ref, e1_ref = (r[lc] for r in router)
        ls = slice(lc * PEER_LANE_CHUNK, (lc + 1) * PEER_LANE_CHUNK)
        i = eb * rows_per_block + il
        r0 = pl.multiple_of(il * PEER_KEYS, PEER_KEYS)
        s = s_scr[pl.ds(r0, PEER_KEYS), ls]
        act = s * (lax.erf(s * (2.0 ** -0.5)) + 1.0) * 0.5
        w = jnp.zeros_like(s)
        for hd in range(PEER_HEADS):
            th = th_ref[hd, pl.ds(i, 1), :]
            cc = c_ref[hd, pl.ds(i, 1), :]
            w = w + jnp.where(s1_ref[hd] >= th, e1_ref[hd], 0.0) * cc
        p_scr[pl.ds(r0, PEER_KEYS), ls] = (w * act).astype(BF16)
        return carry

    for lc in range(nchunk):
        lax.fori_loop(0, rows_per_block, functools.partial(row, lc=lc), 0)
    acc_scr[...] += jnp.dot(vt_ref[...], p_scr[...], preferred_element_type=F32)

    @pl.when(eb == pl.num_programs(1) - 1)
    def _():
        out_ref[...] = y_ref[...] + mod_ref[5:6, :] * acc_scr[...].T


def _peer_dense(tk, y, mod, h, th, c, s1, e1, u_bf, vt_bf):
    d = D_MODEL
    nexp = u_bf.shape[0]
    eb = PEER_EXPERT_BLOCK
    tm = PEER_TOKEN_TILE
    assert tk.nc % tm == 0 and tk.dec_seq % tm == 0
    tok = lambda width: pl.BlockSpec((tm, width), lambda t, e: (t, 0))
    nchunk = tm // PEER_LANE_CHUNK
    rspecs = [pl.BlockSpec((PEER_HEADS, PEER_KEYS, PEER_LANE_CHUNK),
                           functools.partial(lambda t, e, k: (0, 0, nchunk * t + k), k=k)) for k in range(nchunk)]
    return pl.pallas_call(
        _peer_dense_kernel,
        out_shape=jax.ShapeDtypeStruct((tk.n, d), F32),
        grid=(tk.n // tm, nexp // eb),
        in_specs=[tok(d), pl.BlockSpec((None, 6, d), lambda t, e: (tk.cond_row(t, tm), 0, 0)), tok(d),
                  *rspecs, *rspecs, *rspecs, *rspecs,
                  pl.BlockSpec((eb, d), lambda t, e: (e, 0)),
                  pl.BlockSpec((d, eb), lambda t, e: (0, e))],
        out_specs=tok(d),
        scratch_shapes=[pltpu.VMEM((eb, tm), F32), pltpu.VMEM((eb, tm), BF16),
                        pltpu.VMEM((d, tm), F32)],
        compiler_params=_cparams(("parallel", "arbitrary")),
        name="peer_dense",
    )(y, mod, h, *([th] * nchunk), *([c] * nchunk), *([s1] * nchunk), *([e1] * nchunk), u_bf, vt_bf)


def kernel(x_prompt, x_sample, cache_mla_ckv, cache_mla_krope, cache_na_k, cache_na_v, c, c_ctx, w_mod, b_mod, g_norm_mix, g_norm_ffn, mla_w_in, mla_g_q_a, mla_g_kv_a, mla_w_q_b, mla_w_kv_b, mla_g_q, mla_g_k, mla_w_o, na_w_in, na_g_q, na_g_k, na_rel_bias, na_w_o, peer_w_q, peer_sub_keys, peer_u, peer_v):
    batch, seq, d = x_prompt.shape
    dec_batch, dec_seq, _ = x_sample.shape
    past = cache_mla_ckv.shape[2]
    depth = w_mod.shape[0]
    tk = _Tokens(batch, seq, dec_batch, dec_seq)
    assert dec_batch + 1 <= 8 and d == D_MODEL

    y = jnp.concatenate([x_prompt.reshape(tk.nc, d), x_sample.reshape(tk.nl, d)], axis=0)
    cond8 = jnp.zeros((8, d), F32).at[:dec_batch].set(c).at[dec_batch].set(c_ctx)
    mod_all = _adaln(cond8, w_mod, b_mod).reshape(depth, 8, 6, d)
    rope_tabs = _rope_tables(tk)
    rows = dec_seq // GRID_W

    new_ckv, new_krope, new_k, new_v = [], [], [], []
    for i in range(depth):
        j = i // 2
        mod = mod_all[i]
        gmix = g_norm_mix[i].reshape(1, d)
        if i % 2 == 0:
            w = _mla_weights(mla_w_in[j], mla_g_q_a[j], mla_g_kv_a[j], mla_w_q_b[j], mla_w_kv_b[j],
                             mla_g_q[j], mla_g_k[j])
            q, k, v, ckv_n, kr = _mla_proj(tk, y, mod, gmix, w, rope_tabs)
            new_ckv.append(ckv_n[:tk.nc].reshape(batch, seq, MLA_KV_RANK))
            new_krope.append(kr[:tk.nc, MLA_NOPE:MLA_QK].reshape(batch, seq, MLA_ROPE))
            kr_c = jnp.pad(cache_mla_krope[:, j], ((0, 0), (0, 0), (MLA_NOPE, LANES - MLA_QK)))
            kc, vc = _mla_cache_kv(cache_mla_ckv[:, j].reshape(dec_batch * past, MLA_KV_RANK),
                                   kr_c.reshape(dec_batch * past, LANES), w)
            scale = MLA_QK ** -0.5
            o_ctx = _flash(q, k, v, nb=batch, tq_len=seq, tk_len=seq, q_off=0, k_off=0,
                           width=2 * LANES, scale=scale, tq=seq, tkb=seq)
            kfull = jnp.concatenate([kc.reshape(dec_batch, past, -1),
                                     k[tk.nc:].reshape(dec_batch, dec_seq, -1)], axis=1)
            vfull = jnp.concatenate([vc.reshape(dec_batch, past, -1),
                                     v[tk.nc:].reshape(dec_batch, dec_seq, -1)], axis=1)
            klen = past + dec_seq
            o_lat = _flash(q, kfull.reshape(dec_batch * klen, -1), vfull.reshape(dec_batch * klen, -1),
                           nb=dec_batch, tq_len=dec_seq, tk_len=klen, q_off=tk.nc, k_off=0,
                           width=2 * LANES, scale=scale, tq=512, tkb=512)
            w_o = mla_w_o[j].astype(BF16)
        else:
            gq2 = jnp.tile(na_g_q[j], 2).reshape(1, LANES)
            gk2 = jnp.tile(na_g_k[j], 2).reshape(1, LANES)
            q, k, v, kf, vf = _na_proj(tk, y, mod, gmix, na_w_in[j].astype(BF16), gq2, gk2)
            new_k.append(kf[:tk.nc].reshape(batch, seq, NA_HEADS, NA_HD).transpose(0, 2, 1, 3))
            new_v.append(vf[:tk.nc].reshape(batch, seq, NA_HEADS, NA_HD).transpose(0, 2, 1, 3))
            scale = NA_HD ** -0.5
            o_ctx = _flash(q, k, v, nb=batch, tq_len=seq, tk_len=seq, q_off=0, k_off=0,
                           width=LANES, scale=scale, tq=seq, tkb=seq)

            def pair_layout(x):
                b_, h_, l_, e_ = x.shape
                return x.reshape(b_, h_ // 2, 2, l_, e_).transpose(0, 1, 3, 2, 4).reshape(b_, h_ // 2, l_, 2 * e_)

            kc = pair_layout(cache_na_k[:, j]).astype(BF16)
            vc = pair_layout(cache_na_v[:, j]).astype(BF16)
            bias = _na_bias_tables(na_rel_bias[j], rows)
            o_lat = _na_latent(tk, q, k, v, kc, vc, bias)
            w_o = na_w_o[j].astype(BF16)
        o = jnp.concatenate([o_ctx, o_lat], axis=0)
        y = _out_proj(tk, y, o, w_o, mod)

        wqh, wql = _split_bf16(peer_w_q[i])
        skh, skl = _split_bf16(peer_sub_keys[i].reshape(2 * PEER_HEADS, PEER_KEYS, PEER_KEYS))
        h, th, cc, s1, e1 = _peer_router(tk, y, mod, g_norm_ffn[i].reshape(1, d), wqh, wql, skh, skl)
        u_bf = peer_u[i].astype(BF16)
        vt_bf = peer_v[i].T.astype(BF16)
        y = _peer_dense(tk, y, mod, h, th, cc, s1, e1, u_bf, vt_bf)

    return (y[:tk.nc].reshape(batch, seq, d), y[tk.nc:].reshape(dec_batch, dec_seq, d),
            jnp.stack(new_ckv, axis=1), jnp.stack(new_krope, axis=1),
            jnp.stack(new_k, axis=1), jnp.stack(new_v, axis=1))
```
